```python
import jax
import jax.numpy as jnp
from jax import lax
import numpy as np

D_MODEL = 1024
BATCH = 8
SEQ = 2048
DEPTH = 4

D_FF = 2816
NORM_EPS = 1e-6
N_BRANCH = 3

RW_HEADS = 8
RW_HEAD_DIM = 64
RW_WIDTH = RW_HEADS * RW_HEAD_DIM
RW_DECAY_RANK = 64
RW_ICL_RANK = 64
RW_VRES_RANK = 32
RW_GATE_RANK = 128
RW_GN_EPS = 64e-5

CV_CHANNELS = 512
CONV_TAPS = 31
CV_LN_EPS = 1e-5

MLA_HEADS = 8
MLA_Q_RANK = 384
MLA_KV_RANK = 256
MLA_NOPE = 64
MLA_ROPE = 32
MLA_QK = MLA_NOPE + MLA_ROPE
MLA_V = 64
ROPE_THETA = 10000.0
Q_BLOCK = 128

RW_COLS = 3 * RW_WIDTH + 2 * RW_DECAY_RANK + 2 * RW_ICL_RANK + RW_GATE_RANK
CV_COLS = 2 * CV_CHANNELS
MLA_COLS = MLA_Q_RANK + MLA_KV_RANK + MLA_ROPE
GATE_COLS = N_BRANCH * D_MODEL
N_IN = RW_COLS + CV_COLS + MLA_COLS + GATE_COLS

kernel_name = 'hybrid_rwkv7_conformer_mla_encoder'


def rms_norm(x, g, eps=NORM_EPS):
    xf = x.astype(jnp.float32)
    y = xf * lax.rsqrt(jnp.mean(xf * xf, axis=-1, keepdims=True) + eps)
    return (y * g.astype(jnp.float32)).astype(x.dtype)


def layer_norm(x, g, b, eps):
    xf = x.astype(jnp.float32)
    mu = jnp.mean(xf, axis=-1, keepdims=True)
    xc = xf - mu
    var = jnp.mean(xc * xc, axis=-1, keepdims=True)
    y = xc * lax.rsqrt(var + eps) * g.astype(jnp.float32) + b.astype(jnp.float32)
    return y.astype(x.dtype)


def swiglu(x, w1, w3, w2):
    return (jax.nn.silu(x @ w1) * (x @ w3)) @ w2


def centred_shift_mix(p, mu):
    prev = jnp.pad(p[:, :-1], ((0, 0), (1, 0), (0, 0)))
    nxt = jnp.pad(p[:, 1:], ((0, 0), (0, 1), (0, 0)))
    return p + (0.5 * (prev + nxt) - p) * mu


def rwkv7_bidir_scan(r, w, k, v, a, b):
    def to_time_major(t):
        t = jnp.concatenate([t[:1], jnp.flip(t[1:], axis=2)], axis=0)
        return jnp.moveaxis(t, 2, 0)

    def step(state, inp):
        r_t, w_t, k_t, v_t, a_t, b_t = inp
        sa = jnp.einsum('dbhvk,dbhk->dbhv', state, a_t)
        state = (state * w_t[..., None, :] + sa[..., :, None] * b_t[..., None, :]
                 + v_t[..., :, None] * k_t[..., None, :])
        return state, jnp.einsum('dbhvk,dbhk->dbhv', state, r_t)

    xs = tuple(to_time_major(t) for t in (r, w, k, v, a, b))
    s0 = jnp.zeros(xs[0].shape[1:] + (RW_HEAD_DIM,), jnp.float32)
    _, out = lax.scan(step, s0, xs)
    out = jnp.moveaxis(out, 0, 2)
    return out[0] + jnp.flip(out[1], axis=1)


def rwkv7_branch(p, h, v_first, vres, mu, w0, w_up, a0, a_up, g_up, k_k, k_a, r_k,
                 gn_g, gn_b, w_branch):
    B, S, _ = p.shape
    H, N, C = RW_HEADS, RW_HEAD_DIM, RW_WIDTH
    p = centred_shift_mix(p, mu)
    o1 = 3 * C + 2 * RW_DECAY_RANK
    r, k, v, wd, ad, gd = jnp.split(p, [C, 2 * C, 3 * C, o1, o1 + 2 * RW_ICL_RANK], axis=-1)
    wd = jnp.tanh(wd.reshape(B, S, 2, RW_DECAY_RANK))
    ad = ad.reshape(B, S, 2, RW_ICL_RANK)
    w_log = -jax.nn.softplus(-(w0 + jnp.einsum('bsdr,drc->bsdc', wd, w_up)).astype(jnp.float32)) - 0.5
    decay = jnp.exp(-jnp.exp(w_log))
    icl = jax.nn.sigmoid((a0 + jnp.einsum('bsdr,drc->bsdc', ad, a_up)).astype(jnp.float32))
    g = jax.nn.sigmoid(gd) @ g_up
    if vres is None:
        v_first = v
    else:
        v0, v_down, v_up = vres
        nu = jax.nn.sigmoid(v0 + (h @ v_down) @ v_up)
        v = v + (v_first - v) * nu
    rf, kf, vf = (t.astype(jnp.float32) for t in (r, k, v))
    kk = (kf * k_k).reshape(B, S, H, N)
    kk = kk / jnp.maximum(jnp.linalg.norm(kk, axis=-1, keepdims=True), 1e-12)
    kk = kk.reshape(B, S, 1, C)
    k_dir = kf[:, :, None] * (1.0 + (icl - 1.0) * k_a)

    def heads(t):
        t = jnp.broadcast_to(t, (B, S, 2, C)).reshape(B, S, 2, H, N)
        return jnp.moveaxis(t, 2, 0)

    o = rwkv7_bidir_scan(heads(rf[:, :, None]), heads(decay), heads(k_dir),
                         heads(vf[:, :, None]), heads(-kk), heads(kk * icl))
    o = layer_norm(o, gn_g.reshape(H, N), gn_b.reshape(H, N), RW_GN_EPS)
    k_bonus = jnp.mean(k_dir, axis=2).reshape(B, S, H, N)
    bonus = jnp.sum(rf.reshape(B, S, H, N) * k_bonus * r_k, axis=-1, keepdims=True) * vf.reshape(B, S, H, N)
    o = (o + bonus).reshape(B, S, C).astype(h.dtype)
    return (o * g) @ w_branch, v_first


def conformer_conv_branch(p, dw_w, dw_b, ln_g, ln_b, w_branch):
    val, gate = jnp.split(p, 2, axis=-1)
    u = val * jax.nn.sigmoid(gate)
    half = CONV_TAPS // 2
    u = lax.conv_general_dilated(u, dw_w[:, None, :].astype(u.dtype), window_strides=(1,),
                                 padding=[(half, half)], dimension_numbers=('NWC', 'WIO', 'NWC'),
                                 feature_group_count=CV_CHANNELS) + dw_b
    u = jax.nn.silu(layer_norm(u, ln_g, ln_b, CV_LN_EPS))
    return u @ w_branch


def rope_angles(positions):
    inv = ROPE_THETA ** (-jnp.arange(0, MLA_ROPE, 2, dtype=jnp.float32) / MLA_ROPE)
    ang = positions.astype(jnp.float32)[..., None] * inv
    return jnp.cos(ang)[:, :, None, :], jnp.sin(ang)[:, :, None, :]


def apply_rope_tail(t, cos, sin):
    nope = t[..., :MLA_NOPE]
    rot = t[..., MLA_NOPE:].astype(jnp.float32)
    half = MLA_ROPE // 2
    x1, x2 = rot[..., :half], rot[..., half:]
    rot = jnp.concatenate([x1 * cos - x2 * sin, x2 * cos + x1 * sin], axis=-1)
    return jnp.concatenate([nope, rot.astype(t.dtype)], axis=-1)


def blocked_attention(q, k, v):
    B, S, H, Dq = q.shape
    nb = S // Q_BLOCK
    scale = Dq ** -0.5
    qb = jnp.moveaxis(q.reshape(B, nb, Q_BLOCK, H, Dq), 1, 0)

    def attend(q_blk):
        s = jnp.einsum('bqhd,bkhd->bhqk', q_blk, k).astype(jnp.float32) * scale
        pr = jax.nn.softmax(s, axis=-1).astype(v.dtype)
        return jnp.einsum('bhqk,bkhd->bqhd', pr, v)

    o = lax.map(attend, qb)
    return jnp.moveaxis(o, 0, 1).reshape(B, S, H, v.shape[-1])


def mla_branch(p, cos, sin, q_norm, w_uq, kv_norm, w_ukv, qk_q_g, qk_k_g, w_branch):
    B, S, _ = p.shape
    H = MLA_HEADS
    cq, ckv, k_rope = jnp.split(p, [MLA_Q_RANK, MLA_Q_RANK + MLA_KV_RANK], axis=-1)
    q = (rms_norm(cq, q_norm) @ w_uq).reshape(B, S, H, MLA_QK)
    kv = (rms_norm(ckv, kv_norm) @ w_ukv).reshape(B, S, H, MLA_NOPE + MLA_V)
    k_nope, v = kv[..., :MLA_NOPE], kv[..., MLA_NOPE:]
    k = jnp.concatenate([k_nope, jnp.broadcast_to(k_rope[:, :, None], (B, S, H, MLA_ROPE))], axis=-1)
    q = apply_rope_tail(rms_norm(q, qk_q_g), cos, sin)
    k = apply_rope_tail(rms_norm(k, qk_k_g), cos, sin)
    o = blocked_attention(q, k, v)
    return o.reshape(B, S, H * MLA_V) @ w_branch


def setup_inputs(seed: int = 0) -> dict:
    key = jax.random.key(seed)
    ks = iter(jax.random.split(key, 64))
    L = DEPTH

    def nrm(shape, scale):
        return jax.random.normal(next(ks), shape, jnp.float32) * scale

    def gain(shape):
        return 1.0 + nrm(shape, 0.02)

    D, F = D_MODEL, D_FF
    x = jax.random.normal(next(ks), (BATCH, SEQ, D), jnp.float32)
    positions = (jnp.arange(SEQ, dtype=jnp.int32)[None, :]
                 + jax.random.randint(next(ks), (BATCH, 1), 0, 1024, dtype=jnp.int32))
    return {
        'x': x,
        'positions': positions,
        'norm_ffn1': gain((L, D)),
        'ffn1_w1': nrm((L, D, F), D ** -0.5),
        'ffn1_w3': nrm((L, D, F), D ** -0.5),
        'ffn1_w2': nrm((L, F, D), F ** -0.5),
        'norm_mix': gain((L, D)),
        'w_in': nrm((L, D, N_IN), D ** -0.5),
        'rw_mu': jax.random.uniform(next(ks), (L, RW_COLS), jnp.float32),
        'rw_w0': jax.random.uniform(next(ks), (L, 2, RW_WIDTH), jnp.float32, -4.0, 0.0),
        'rw_w_up': nrm((L, 2, RW_DECAY_RANK, RW_WIDTH), 0.1),
        'rw_a0': nrm((L, 2, RW_WIDTH), 0.1),
        'rw_a_up': nrm((L, 2, RW_ICL_RANK, RW_WIDTH), 0.1),
        'rw_g_up': nrm((L, RW_GATE_RANK, RW_WIDTH), RW_GATE_RANK ** -0.5),
        'rw_k_k': 0.85 + nrm((L, RW_WIDTH), 0.05),
        'rw_k_a': 1.0 + nrm((L, RW_WIDTH), 0.05),
        'rw_r_k': nrm((L, RW_HEADS, RW_HEAD_DIM), 0.1),
        'rw_v0': nrm((L - 1, RW_WIDTH), 0.1),
        'rw_v_down': nrm((L - 1, D, RW_VRES_RANK), D ** -0.5),
        'rw_v_up': nrm((L - 1, RW_VRES_RANK, RW_WIDTH), RW_VRES_RANK ** -0.5),
        'rw_gn_g': gain((L, RW_WIDTH)),
        'rw_gn_b': nrm((L, RW_WIDTH), 0.02),
        'rw_w_branch': nrm((L, RW_WIDTH, D), RW_WIDTH ** -0.5),
        'cv_dw_w': nrm((L, CONV_TAPS, CV_CHANNELS), CONV_TAPS ** -0.5),
        'cv_dw_b': nrm((L, CV_CHANNELS), 0.02),
        'cv_ln_g': gain((L, CV_CHANNELS)),
        'cv_ln_b': nrm((L, CV_CHANNELS), 0.02),
        'cv_w_branch': nrm((L, CV_CHANNELS, D), CV_CHANNELS ** -0.5),
        'mla_q_norm': gain((L, MLA_Q_RANK)),
        'mla_w_uq': nrm((L, MLA_Q_RANK, MLA_HEADS * MLA_QK), MLA_Q_RANK ** -0.5),
        'mla_kv_norm': gain((L, MLA_KV_RANK)),
        'mla_w_ukv': nrm((L, MLA_KV_RANK, MLA_HEADS * (MLA_NOPE + MLA_V)), MLA_KV_RANK ** -0.5),
        'mla_qk_q_g': gain((L, MLA_QK)),
        'mla_qk_k_g': gain((L, MLA_QK)),
        'mla_w_branch': nrm((L, MLA_HEADS * MLA_V, D), (MLA_HEADS * MLA_V) ** -0.5),
        'w_o': nrm((L, D, D), D ** -0.5),
        'norm_ffn2': gain((L, D)),
        'ffn2_w1': nrm((L, D, F), D ** -0.5),
        'ffn2_w3': nrm((L, D, F), D ** -0.5),
        'ffn2_w2': nrm((L, F, D), F ** -0.5),
    }


def reference(x, positions,
              norm_ffn1, ffn1_w1, ffn1_w3, ffn1_w2,
              norm_mix, w_in,
              rw_mu, rw_w0, rw_w_up, rw_a0, rw_a_up, rw_g_up, rw_k_k, rw_k_a, rw_r_k,
              rw_v0, rw_v_down, rw_v_up, rw_gn_g, rw_gn_b, rw_w_branch,
              cv_dw_w, cv_dw_b, cv_ln_g, cv_ln_b, cv_w_branch,
              mla_q_norm, mla_w_uq, mla_kv_norm, mla_w_ukv, mla_qk_q_g, mla_qk_k_g, mla_w_branch,
              w_o,
              norm_ffn2, ffn2_w1, ffn2_w3, ffn2_w2):
    B, S, D = x.shape
    cos, sin = rope_angles(positions)
    splits = [RW_COLS, RW_COLS + CV_COLS, RW_COLS + CV_COLS + MLA_COLS]
    v_first = None
    for i in range(DEPTH):
        x = x + 0.5 * swiglu(rms_norm(x, norm_ffn1[i]), ffn1_w1[i], ffn1_w3[i], ffn1_w2[i])
        h = rms_norm(x, norm_mix[i])
        proj = h @ w_in[i]
        p_rw, p_cv, p_mla, p_gate = jnp.split(proj, splits, axis=-1)
        vres = None if i == 0 else (rw_v0[i - 1], rw_v_down[i - 1], rw_v_up[i - 1])
        y_rw, v_first = rwkv7_branch(p_rw, h, v_first, vres, rw_mu[i], rw_w0[i], rw_w_up[i],
                                     rw_a0[i], rw_a_up[i], rw_g_up[i], rw_k_k[i], rw_k_a[i],
                                     rw_r_k[i], rw_gn_g[i], rw_gn_b[i], rw_w_branch[i])
        y_cv = conformer_conv_branch(p_cv, cv_dw_w[i], cv_dw_b[i], cv_ln_g[i], cv_ln_b[i],
                                     cv_w_branch[i])
        y_mla = mla_branch(p_mla, cos, sin, mla_q_norm[i], mla_w_uq[i], mla_kv_norm[i],
                           mla_w_ukv[i], mla_qk_q_g[i], mla_qk_k_g[i], mla_w_branch[i])
        gates = jax.nn.sigmoid(p_gate).reshape(B, S, N_BRANCH, D)
        merged = gates[:, :, 0] * y_rw + gates[:, :, 1] * y_cv + gates[:, :, 2] * y_mla
        x = x + merged @ w_o[i]
        x = x + 0.5 * swiglu(rms_norm(x, norm_ffn2[i]), ffn2_w1[i], ffn2_w3[i], ffn2_w2[i])
    return x
```

```python
import functools
import math

import jax
import jax.numpy as jnp
from jax import lax
from jax.experimental import pallas as pl
from jax.experimental.pallas import tpu as pltpu

F32 = jnp.float32
BF16 = jnp.bfloat16

D_MODEL = 1024
D_FF = 2816
NORM_EPS = 1e-6

RW_HEADS = 8
RW_HEAD_DIM = 64
RW_WIDTH = RW_HEADS * RW_HEAD_DIM
RW_DECAY_RANK = 64
RW_ICL_RANK = 64
RW_VRES_RANK = 32
RW_GATE_RANK = 128
RW_GN_EPS = 64e-5
RW_COLS = 3 * RW_WIDTH + 2 * RW_DECAY_RANK + 2 * RW_ICL_RANK + RW_GATE_RANK

CV_CHANNELS = 512
CONV_TAPS = 31
CV_LN_EPS = 1e-5
CV_COLS = 2 * CV_CHANNELS

MLA_HEADS = 8
MLA_Q_RANK = 384
MLA_KV_RANK = 256
MLA_NOPE = 64
MLA_ROPE = 32
MLA_QK = MLA_NOPE + MLA_ROPE
MLA_V = 64
ROPE_THETA = 10000.0
MLA_COLS = MLA_Q_RANK + MLA_KV_RANK + MLA_ROPE
MLA_COLS_PAD = 768
GATE_COLS = 3 * D_MODEL
VD_COLS_PAD = 128

LANE = 128
VMEM_LIMIT = 56 * 1024 * 1024

RW_CHUNK = 64


def _cparams(sem):
    return pltpu.CompilerParams(dimension_semantics=sem, vmem_limit_bytes=VMEM_LIMIT)


def _const_spec(shape):
    n = len(shape)
    return pl.BlockSpec(shape, lambda *_: (0,) * n)


def _mm(a, b):
    return jnp.dot(a.astype(BF16), b.astype(BF16), preferred_element_type=F32)


def _mm_nt(a, b):
    return lax.dot_general(a.astype(BF16), b.astype(BF16), (((1,), (1,)), ((), ())),
                           preferred_element_type=F32)


def _mm_tn(a, b):
    return lax.dot_general(a.astype(BF16), b.astype(BF16), (((0,), (0,)), ((), ())),
                           preferred_element_type=F32)


def _split3(x):
    hi = x.astype(BF16)
    r1 = x - hi.astype(F32)
    mid = r1.astype(BF16)
    lo = (r1 - mid.astype(F32)).astype(BF16)
    return hi, mid, lo


def _mm_x_exact(x, m):
    hi, mid, lo = _split3(x)
    return (jnp.dot(hi, m, preferred_element_type=F32) + jnp.dot(mid, m, preferred_element_type=F32)
            + jnp.dot(lo, m, preferred_element_type=F32))


def _mm_exact_x(m, x):
    hi, mid, lo = _split3(x)
    return (jnp.dot(m, hi, preferred_element_type=F32) + jnp.dot(m, mid, preferred_element_type=F32)
            + jnp.dot(m, lo, preferred_element_type=F32))


def _sigmoid(x):
    return 1.0 / (1.0 + jnp.exp(-x))


def _rms(x, g, eps=NORM_EPS):
    return x * lax.rsqrt(jnp.mean(x * x, axis=-1, keepdims=True) + eps) * g


def _ffn_kernel(x_ref, g_ref, w1_ref, w3_ref, w2_ref, o_ref, h_ref):
    j = pl.program_id(1)

    @pl.when(j == 0)
    def _():
        x = x_ref[...]
        h_ref[...] = _rms(x, g_ref[...]).astype(BF16)
        o_ref[...] = x

    h = h_ref[...]
    a = jnp.dot(h, w1_ref[...], preferred_element_type=F32)
    b = jnp.dot(h, w3_ref[...], preferred_element_type=F32)
    u = (a * _sigmoid(a) * b).astype(BF16)
    o_ref[...] += 0.5 * jnp.dot(u, w2_ref[...], preferred_element_type=F32)


def _ffn(x2, g, w1, w3, w2, *, tm, tf):
    T, D = x2.shape
    F = w1.shape[1]
    return pl.pallas_call(
        _ffn_kernel,
        grid=(T // tm, F // tf),
        in_specs=[
            pl.BlockSpec((tm, D), lambda i, j: (i, 0)),
            pl.BlockSpec((1, D), lambda i, j: (0, 0)),
            pl.BlockSpec((D, tf), lambda i, j: (0, j)),
            pl.BlockSpec((D, tf), lambda i, j: (0, j)),
            pl.BlockSpec((tf, D), lambda i, j: (j, 0)),
        ],
        out_specs=pl.BlockSpec((tm, D), lambda i, j: (i, 0)),
        out_shape=jax.ShapeDtypeStruct((T, D), F32),
        scratch_shapes=[pltpu.VMEM((tm, D), BF16)],
        compiler_params=_cparams(("parallel", "arbitrary")),
        name="ffn",
    )(x2, g, w1, w3, w2)


def _inproj_kernel(x_ref, g_ref, w_ref, rw_ref, cv_ref, mla_ref, gate_ref, vd_ref):
    h = _rms(x_ref[...], g_ref[...]).astype(BF16)
    c = 0
    for ref in (rw_ref, cv_ref, mla_ref, gate_ref, vd_ref):
        n = ref.shape[1]
        ref[...] = jnp.dot(h, w_ref[:, c:c + n], preferred_element_type=F32).astype(ref.dtype)
        c += n


def _inproj(x2, g, w_cat, *, tm):
    T, D = x2.shape
    widths = (RW_COLS, CV_COLS, MLA_COLS_PAD, GATE_COLS, VD_COLS_PAD)
    assert w_cat.shape[1] == sum(widths)
    return pl.pallas_call(
        _inproj_kernel,
        grid=(T // tm,),
        in_specs=[
            pl.BlockSpec((tm, D), lambda i: (i, 0)),
            _const_spec((1, D)),
            _const_spec(w_cat.shape),
        ],
        out_specs=[pl.BlockSpec((tm, n), lambda i: (i, 0)) for n in widths],
        out_shape=[jax.ShapeDtypeStruct((T, n), F32) for n in widths],
        compiler_params=_cparams(("parallel",)),
        name="inproj",
    )(x2, g, w_cat)


def _softplus(z):
    return jnp.maximum(z, 0.0) + jnp.log(1.0 + jnp.exp(-jnp.abs(z)))


def _rw_prep_kernel(*refs, has_vres):
    if has_vres:
        (p_ref, pp_ref, pn_ref, vd_ref, vf_ref, mu_ref, wup_ref, w0_ref, aup_ref, a0_ref, gup_ref,
         kk_ref, ka_ref, rk_ref, e_ref, v0_ref, vup_ref,
         r_o, v_o, kn_o, lw_o, kd_o, ic_o, g_o, bn_o) = refs
    else:
        (p_ref, pp_ref, pn_ref, mu_ref, wup_ref, w0_ref, aup_ref, a0_ref, gup_ref,
         kk_ref, ka_ref, rk_ref, e_ref,
         r_o, v_o, kn_o, lw_o, kd_o, ic_o, g_o, bn_o) = refs
    i = pl.program_id(1)
    n = pl.num_programs(1)
    C = RW_WIDTH
    p = p_ref[0]
    ts = p.shape[0]
    prev_row = jnp.where(i > 0, pp_ref[0, 7:8, :], 0.0)
    next_row = jnp.where(i < n - 1, pn_ref[0, 0:1, :], 0.0)
    rows = lax.broadcasted_iota(jnp.int32, (ts, 1), 0)
    prev = jnp.where(rows == 0, prev_row, pltpu.roll(p, 1, 0))
    nxt = jnp.where(rows == ts - 1, next_row, pltpu.roll(p, ts - 1, 0))
    pm = p + (0.5 * (prev + nxt) - p) * mu_ref[...]

    r = pm[:, 0:C]
    k = pm[:, C:2 * C]
    v = pm[:, 2 * C:3 * C]
    o1 = 3 * C
    wd = jnp.tanh(pm[:, o1:o1 + 2 * RW_DECAY_RANK])
    o2 = o1 + 2 * RW_DECAY_RANK
    ad = pm[:, o2:o2 + 2 * RW_ICL_RANK]
    o3 = o2 + 2 * RW_ICL_RANK
    gd = pm[:, o3:o3 + RW_GATE_RANK]

    wl = w0_ref[...] + _mm(wd, wup_ref[...])
    lw = -jnp.exp(-_softplus(-wl) - 0.5)
    icl = _sigmoid(a0_ref[...] + _mm(ad, aup_ref[...]))
    g = _mm(_sigmoid(gd), gup_ref[...])

    if has_vres:
        nu = _sigmoid(v0_ref[...] + _mm(vd_ref[...], vup_ref[...]))
        v = v + (vf_ref[0] - v) * nu

    e = e_ref[...]
    kkr = k * kk_ref[...]
    ss = _mm_x_exact(kkr * kkr, e)
    kn = kkr * lax.rsqrt(jnp.maximum(ss, 1e-24))

    ka = ka_ref[...]
    kd0 = k * (1.0 + (icl[:, 0:C] - 1.0) * ka)
    kd1 = k * (1.0 + (icl[:, C:2 * C] - 1.0) * ka)
    kb = 0.5 * (kd0 + kd1)
    sb = _mm_x_exact(r * kb * rk_ref[...], e)

    r_o[0] = r
    v_o[0] = v
    kn_o[0] = kn
    lw_o[0, 0] = lw[:, 0:C]
    lw_o[0, 1] = lw[:, C:2 * C]
    kd_o[0, 0] = kd0
    kd_o[0, 1] = kd1
    ic_o[0, 0] = icl[:, 0:C]
    ic_o[0, 1] = icl[:, C:2 * C]
    g_o[0] = g
    bn_o[0] = sb * v


def _rw_prep(p_rw, vd, v_first, wts, *, ts):
    B, S, _ = p_rw.shape
    C = RW_WIDTH
    has_vres = vd is not None
    nblk8 = S // 8
    tile = lambda b, i: (b, i, 0)
    in_arrays = [p_rw, p_rw, p_rw]
    in_specs = [
        pl.BlockSpec((1, ts, RW_COLS), tile),
        pl.BlockSpec((1, 8, RW_COLS), lambda b, i: (b, jnp.maximum(i * (ts // 8) - 1, 0), 0)),
        pl.BlockSpec((1, 8, RW_COLS), lambda b, i: (b, jnp.minimum((i + 1) * (ts // 8), nblk8 - 1), 0)),
    ]
    if has_vres:
        in_arrays += [vd, v_first]
        in_specs += [pl.BlockSpec((ts, VD_COLS_PAD), lambda b, i: (b * (S // ts) + i, 0)),
                     pl.BlockSpec((1, ts, C), tile)]
    names = ["mu", "wup", "w0", "aup", "a0", "gup", "k_k", "k_a", "r_k", "e64"]
    if has_vres:
        names += ["v0", "vup"]
    for nme in names:
        in_arrays.append(wts[nme])
        in_specs.append(_const_spec(wts[nme].shape))
    one = jax.ShapeDtypeStruct((B, S, C), F32)
    two = jax.ShapeDtypeStruct((B, 2, S, C), F32)
    ospec1 = pl.BlockSpec((1, ts, C), tile)
    ospec2 = pl.BlockSpec((1, 2, ts, C), lambda b, i: (b, 0, i, 0))
    return pl.pallas_call(
        functools.partial(_rw_prep_kernel, has_vres=has_vres),
        grid=(B, S // ts),
        in_specs=in_specs,
        out_specs=[ospec1, ospec1, ospec1, ospec2, ospec2, ospec2, ospec1, ospec1],
        out_shape=[one, one, one, two, two, two, one, one],
        compiler_params=_cparams(("parallel", "parallel")),
        name="rw_prep",
    )(*in_arrays)


def _rw_scan_kernel(r_ref, v_ref, kn_ref, lw_ref, kd_ref, ic_ref, o_ref, z_ref, *, n_chunks):
    d = pl.program_id(1)
    j = pl.program_id(2)
    C = RW_CHUNK
    N = RW_HEAD_DIM

    @pl.when(j == 0)
    def _():
        z_ref[...] = jnp.zeros_like(z_ref)

    ti = lax.broadcasted_iota(jnp.int32, (C, C), 0)
    ii = lax.broadcasted_iota(jnp.int32, (C, C), 1)
    order = (ti - ii) * (1 - 2 * d)
    incl = order >= 0
    strict = order > 0
    tri = jnp.where(incl, 1.0, 0.0).astype(BF16)
    eye = jnp.where(ii == ti, 1.0, 0.0)

    def chunk(cc, carry):
        c_idx = cc + d * (n_chunks - 1 - 2 * cc)
        start = pl.multiple_of(c_idx * C, C)
        rows = pl.ds(start, C)
        r = r_ref[0, rows, :]
        v = v_ref[0, rows, :]
        kn = kn_ref[0, rows, :]
        lw = lw_ref[0, 0, rows, :]
        kd = kd_ref[0, 0, rows, :]
        ic = ic_ref[0, 0, rows, :]

        lc = _mm_exact_x(tri, lw)
        le = lc - lw
        ltot = jnp.sum(lw, axis=0, keepdims=True)
        e_lc = jnp.exp(lc)
        e_nlc = jnp.exp(-lc)
        e_t = jnp.exp(ltot - lc)
        gam = jnp.exp(ltot)
        b = kn * ic
        at = -kn * jnp.exp(le)
        rt = r * e_lc
        bt = b * e_nlc
        kt = kd * e_nlc
        bh = b * e_t
        kh = kd * e_t

        outs = []
        for h in range(RW_HEADS):
            sl = slice(h * N, (h + 1) * N)
            at_h = at[:, sl]
            rt_h = rt[:, sl]
            v_h = v[:, sl]
            ar = jnp.concatenate([at_h, rt_h], axis=0)
            gb = _mm_nt(ar, bt[:, sl])
            gk = _mm_nt(ar, kt[:, sl])
            lab = jnp.where(strict, gb[:C], 0.0)
            mrb = jnp.where(incl, gb[C:], 0.0)
            lak = jnp.where(strict, gk[:C], 0.0)
            mrk = jnp.where(incl, gk[C:], 0.0)
            tinv = eye + lab
            lp = lab
            for _ in range(int(math.log2(C)) - 1):
                lp = _mm(lp, lp)
                tinv = tinv + _mm(tinv, lp)
            ah = _mm(tinv, at_h)
            vh = _mm(tinv, _mm(lak, v_h))
            z = z_ref[h]
            u = _mm_nt(ah, z) + vh
            o = _mm_nt(rt_h, z) + _mm(mrb, u) + _mm(mrk, v_h)
            z_ref[h] = z * gam[:, sl] + _mm_tn(u, bh[:, sl]) + _mm_tn(v_h, kh[:, sl])
            outs.append(o)
        o_ref[0, 0, rows, :] = jnp.concatenate(outs, axis=1)
        return carry

    lax.fori_loop(0, n_chunks, chunk, 0)


def _rw_scan(r, v, kn, lw, kd, ic, *, tt):
    B, S, C = r.shape
    nT = S // tt
    tile1 = lambda b, d, j: (b, j + d * (nT - 1 - 2 * j), 0)
    tile2 = lambda b, d, j: (b, d, j + d * (nT - 1 - 2 * j), 0)
    tile_o = lambda b, d, j: (d, b, j + d * (nT - 1 - 2 * j), 0)
    s1 = pl.BlockSpec((1, tt, C), tile1)
    s2 = pl.BlockSpec((1, 1, tt, C), tile2)
    return pl.pallas_call(
        functools.partial(_rw_scan_kernel, n_chunks=tt // RW_CHUNK),
        grid=(B, 2, nT),
        in_specs=[s1, s1, s1, s2, s2, s2],
        out_specs=pl.BlockSpec((1, 1, tt, C), tile_o),
        out_shape=jax.ShapeDtypeStruct((2, B, S, C), F32),
        scratch_shapes=[pltpu.VMEM((RW_HEADS, RW_HEAD_DIM, RW_HEAD_DIM), F32)],
        compiler_params=_cparams(("parallel", "parallel", "arbitrary")),
        name="rw_scan",
    )(r, v, kn, lw, kd, ic)


CV_HALO = 16


def _cv_kernel(p_ref, pp_ref, pn_ref, dw_ref, db_ref, lg_ref, lb_ref, o_ref, u_ref):
    i = pl.program_id(1)
    n = pl.num_programs(1)
    ts = p_ref.shape[1]
    Cc = CV_CHANNELS

    def glu(x):
        return x[:, :Cc] * _sigmoid(x[:, Cc:])

    u_ref[CV_HALO:CV_HALO + ts, :] = glu(p_ref[0])
    u_ref[0:CV_HALO, :] = jnp.where(i > 0, glu(pp_ref[0]), 0.0)
    u_ref[CV_HALO + ts:, :] = jnp.where(i < n - 1, glu(pn_ref[0]), 0.0)
    half = CONV_TAPS // 2
    acc = jnp.zeros((ts, Cc), F32)
    for t in range(CONV_TAPS):
        off = CV_HALO - half + t
        acc = acc + u_ref[off:off + ts, :] * dw_ref[t:t + 1, :]
    acc = acc + db_ref[...]
    mu = jnp.mean(acc, axis=-1, keepdims=True)
    xc = acc - mu
    var = jnp.mean(xc * xc, axis=-1, keepdims=True)
    y = xc * lax.rsqrt(var + CV_LN_EPS) * lg_ref[...] + lb_ref[...]
    o_ref[0] = (y * _sigmoid(y)).astype(o_ref.dtype)


def _cv(p_cv, dw, db, lg, lb, *, ts):
    B, S, _ = p_cv.shape
    nh = S // CV_HALO
    return pl.pallas_call(
        _cv_kernel,
        grid=(B, S // ts),
        in_specs=[
            pl.BlockSpec((1, ts, CV_COLS), lambda b, i: (b, i, 0)),
            pl.BlockSpec((1, CV_HALO, CV_COLS),
                         lambda b, i: (b, jnp.maximum(i * (ts // CV_HALO) - 1, 0), 0)),
            pl.BlockSpec((1, CV_HALO, CV_COLS),
                         lambda b, i: (b, jnp.minimum((i + 1) * (ts // CV_HALO), nh - 1), 0)),
            _const_spec(dw.shape), _const_spec(db.shape), _const_spec(lg.shape), _const_spec(lb.shape),
        ],
        out_specs=pl.BlockSpec((1, ts, CV_CHANNELS), lambda b, i: (b, i, 0)),
        out_shape=jax.ShapeDtypeStruct((B, S, CV_CHANNELS), BF16),
        scratch_shapes=[pltpu.VMEM((ts + 2 * CV_HALO, CV_CHANNELS), F32)],
        compiler_params=_cparams(("parallel", "parallel")),
        name="cv_branch",
    )(p_cv, p_cv, p_cv, dw, db, lg, lb)


def _mla_prep_kernel(p_ref, cb_ref, s1_ref, s2_ref, qn_ref, wq_ref, kvn_ref, wkv_ref, gq_ref, gk_ref,
                     q_o, k_o, v_o):
    p = p_ref[0]
    cb = cb_ref[0]
    s1 = s1_ref[0]
    s2 = s2_ref[0]
    cq = p[:, 0:MLA_Q_RANK]
    ckv = p[:, MLA_Q_RANK:MLA_Q_RANK + MLA_KV_RANK]
    kr = p[:, MLA_Q_RANK + MLA_KV_RANK:MLA_COLS_PAD]
    qall = _mm(_rms(cq, qn_ref[...]), wq_ref[...])
    kvall = _mm(_rms(ckv, kvn_ref[...]), wkv_ref[...])
    krot = pltpu.roll(kr, MLA_NOPE, 1)
    gq = gq_ref[...]
    gk = gk_ref[...]
    lane = lax.broadcasted_iota(jnp.int32, (1, LANE), 1)
    ones_col = jnp.where(lane == MLA_V, 1.0, 0.0)
    scale = MLA_QK ** -0.5
    HB = MLA_HEADS * LANE

    def rope(x):
        return x * cb + pltpu.roll(x, LANE - MLA_ROPE // 2, 1) * s1 + pltpu.roll(x, MLA_ROPE // 2, 1) * s2

    for h in range(MLA_HEADS):
        qb = qall[:, h * LANE:(h + 1) * LANE]
        rq = lax.rsqrt(jnp.sum(qb * qb, axis=-1, keepdims=True) * (1.0 / MLA_QK) + NORM_EPS)
        q_o[0, h] = (rope(qb * gq) * (rq * scale)).astype(q_o.dtype)
        kb = kvall[:, h * LANE:(h + 1) * LANE] + krot
        rk = lax.rsqrt(jnp.sum(kb * kb, axis=-1, keepdims=True) * (1.0 / MLA_QK) + NORM_EPS)
        k_o[0, h] = (rope(kb * gk) * rk).astype(k_o.dtype)
        v_o[0, h] = (kvall[:, HB + h * LANE:HB + (h + 1) * LANE] + ones_col).astype(v_o.dtype)


def _mla_prep(p_mla, tabs, wts, *, ts):
    B, S, _ = p_mla.shape
    H = MLA_HEADS
    tile = lambda b, i: (b, i, 0)
    tspec = pl.BlockSpec((1, ts, LANE), tile)
    names = ["q_norm", "wq", "kv_norm", "wkv", "gq", "gk"]
    hspec = pl.BlockSpec((1, H, ts, LANE), lambda b, i: (b, 0, i, 0))
    hshape = jax.ShapeDtypeStruct((B, H, S, LANE), BF16)
    return pl.pallas_call(
        _mla_prep_kernel,
        grid=(B, S // ts),
        in_specs=[pl.BlockSpec((1, ts, MLA_COLS_PAD), tile), tspec, tspec, tspec]
        + [_const_spec(wts[n].shape) for n in names],
        out_specs=[hspec, hspec, hspec],
        out_shape=[hshape, hshape, hshape],
        compiler_params=_cparams(("parallel", "parallel")),
        name="mla_prep",
    )(p_mla, *tabs, *[wts[n] for n in names])


def _attn_kernel(q_ref, k_ref, v_ref, o_ref):
    outs = []
    for h in range(2):
        s = _mm_nt(q_ref[0, h], k_ref[0, h])
        m = jnp.max(s, axis=-1, keepdims=True)
        p = jnp.exp(s - m).astype(BF16)
        ov = jnp.dot(p, v_ref[0, h], preferred_element_type=F32)
        outs.append(ov[:, :MLA_V] / ov[:, MLA_V:MLA_V + 1])
    o_ref[0] = jnp.concatenate(outs, axis=1).astype(o_ref.dtype)


def _attn(q, k, v, *, tq):
    B, H, S, _ = q.shape
    return pl.pallas_call(
        _attn_kernel,
        grid=(B, H // 2, S // tq),
        in_specs=[
            pl.BlockSpec((1, 2, tq, LANE), lambda b, h, i: (b, h, i, 0)),
            pl.BlockSpec((1, 2, S, LANE), lambda b, h, i: (b, h, 0, 0)),
            pl.BlockSpec((1, 2, S, LANE), lambda b, h, i: (b, h, 0, 0)),
        ],
        out_specs=pl.BlockSpec((1, tq, 2 * MLA_V), lambda b, h, i: (b, i, h)),
        out_shape=jax.ShapeDtypeStruct((B, S, H * MLA_V), BF16),
        compiler_params=_cparams(("parallel", "parallel", "parallel")),
        name="mla_attn",
    )(q, k, v)


def _merge_kernel(x_ref, o2_ref, g_ref, bn_ref, ucv_ref, omla_ref, gate_ref,
                  avg_ref, gng_ref, gnb_ref, wrw_ref, wcv_ref, wmla_ref, wo_ref, out_ref):
    D = D_MODEL
    o = o2_ref[0] + o2_ref[1]
    avg = avg_ref[...]
    inv_n = 1.0 / RW_HEAD_DIM
    mu = _mm_x_exact(o, avg) * inv_n
    xc = o - mu
    var = _mm_x_exact(xc * xc, avg) * inv_n
    on = xc * lax.rsqrt(var + RW_GN_EPS) * gng_ref[...] + gnb_ref[...]
    orw = (on + bn_ref[...]) * g_ref[...]
    gate = gate_ref[...]
    merged = (_sigmoid(gate[:, 0:D]) * _mm(orw, wrw_ref[...])
              + _sigmoid(gate[:, D:2 * D]) * jnp.dot(ucv_ref[...], wcv_ref[...], preferred_element_type=F32)
              + _sigmoid(gate[:, 2 * D:3 * D]) * jnp.dot(omla_ref[...], wmla_ref[...],
                                                         preferred_element_type=F32))
    out_ref[...] = x_ref[...] + _mm(merged, wo_ref[...])


def _merge(x2, o2, g, bn, ucv, omla, gate, wts, *, tm):
    T, D = x2.shape
    C = RW_WIDTH
    row = lambda n: pl.BlockSpec((tm, n), lambda i: (i, 0))
    names = ["e64", "gn_g", "gn_b", "w_rw", "w_cv", "w_mla", "w_o"]
    return pl.pallas_call(
        _merge_kernel,
        grid=(T // tm,),
        in_specs=[row(D), pl.BlockSpec((2, tm, C), lambda i: (0, i, 0)), row(C), row(C), row(C), row(C),
                  row(GATE_COLS)] + [_const_spec(wts[n].shape) for n in names],
        out_specs=row(D),
        out_shape=jax.ShapeDtypeStruct((T, D), F32),
        compiler_params=_cparams(("parallel",)),
        name="merge",
    )(x2, o2, g, bn, ucv, omla, gate, *[wts[n] for n in names])


def _block_diag2(w):
    z = jnp.zeros_like(w[0])
    return jnp.concatenate([jnp.concatenate([w[0], z], axis=1), jnp.concatenate([z, w[1]], axis=1)], axis=0)


def _head_block_cols(w, width):
    R = w.shape[0]
    w = w.reshape(R, MLA_HEADS, width)
    return jnp.pad(w, ((0, 0), (0, 0), (0, LANE - width))).reshape(R, MLA_HEADS * LANE)


def _rope_tables(positions):
    half = MLA_ROPE // 2
    inv = ROPE_THETA ** (-jnp.arange(0, MLA_ROPE, 2, dtype=F32) / MLA_ROPE)
    ang = positions.astype(F32)[..., None] * inv
    cos, sin = jnp.cos(ang), jnp.sin(ang)
    B, S = positions.shape
    one = jnp.ones((B, S, MLA_NOPE), F32)
    z64 = jnp.zeros((B, S, MLA_NOPE), F32)
    z16 = jnp.zeros((B, S, half), F32)
    pad = jnp.zeros((B, S, LANE - MLA_QK), F32)
    cb = jnp.concatenate([one, cos, cos, pad + 1.0], axis=-1)
    s1 = jnp.concatenate([z64, -sin, z16, pad], axis=-1)
    s2 = jnp.concatenate([z64, z16, sin, pad], axis=-1)
    return cb, s1, s2


def _head_ones():
    hid = jnp.arange(RW_WIDTH) // RW_HEAD_DIM
    return (hid[:, None] == hid[None, :]).astype(BF16)


def _tile_sizes(T, S):
    pick = lambda n, cands: next(c for c in cands if n % c == 0)
    return dict(
        ffn_tm=pick(T, (1024, 512, 256, 128, 64, 32, 16, 8)),
        ffn_tf=pick(D_FF, (256, 128)),
        tok_tm=pick(T, (512, 256, 128, 64, 32, 16, 8)),
        seq_ts=pick(S, (512, 256, 128, 64)),
        scan_tt=pick(S, (512, 256, 128, 64)),
        attn_tq=pick(S, (512, 256, 128, 64)),
    )


def kernel(x, positions, norm_ffn1, ffn1_w1, ffn1_w3, ffn1_w2, norm_mix, w_in, rw_mu, rw_w0, rw_w_up, rw_a0, rw_a_up, rw_g_up, rw_k_k, rw_k_a, rw_r_k, rw_v0, rw_v_down, rw_v_up, rw_gn_g, rw_gn_b, rw_w_branch, cv_dw_w, cv_dw_b, cv_ln_g, cv_ln_b, cv_w_branch, mla_q_norm, mla_w_uq, mla_kv_norm, mla_w_ukv, mla_qk_q_g, mla_qk_k_g, mla_w_branch, w_o, norm_ffn2, ffn2_w1, ffn2_w3, ffn2_w2):
    B, S, D = x.shape
    T = B * S
    L = w_in.shape[0]
    ts_ = _tile_sizes(T, S)
    tabs = _rope_tables(positions)
    e64 = _head_ones()
    row = lambda a: a.reshape(1, -1)
    x2 = x.reshape(T, D)
    v_first = None
    for i in range(L):
        x2 = _ffn(x2, row(norm_ffn1[i]), ffn1_w1[i].astype(BF16), ffn1_w3[i].astype(BF16),
                  ffn1_w2[i].astype(BF16), tm=ts_["ffn_tm"], tf=ts_["ffn_tf"])

        c1, c2, c3 = RW_COLS, RW_COLS + CV_COLS, RW_COLS + CV_COLS + MLA_COLS
        wi = w_in[i]
        vdw = rw_v_down[i - 1] if i > 0 else jnp.zeros((D, RW_VRES_RANK), F32)
        w_cat = jnp.concatenate([
            wi[:, :c1], wi[:, c1:c2],
            jnp.pad(wi[:, c2:c3], ((0, 0), (0, MLA_COLS_PAD - MLA_COLS))),
            wi[:, c3:], jnp.pad(vdw, ((0, 0), (0, VD_COLS_PAD - RW_VRES_RANK)))], axis=1).astype(BF16)
        p_rw, p_cv, p_mla, p_gate, p_vd = _inproj(x2, row(norm_mix[i]), w_cat, tm=ts_["tok_tm"])

        rw_wts = dict(
            mu=row(rw_mu[i]), wup=_block_diag2(rw_w_up[i]).astype(BF16), w0=row(rw_w0[i]),
            aup=_block_diag2(rw_a_up[i]).astype(BF16), a0=row(rw_a0[i]), gup=rw_g_up[i].astype(BF16),
            k_k=row(rw_k_k[i]), k_a=row(rw_k_a[i]), r_k=row(rw_r_k[i]), e64=e64)
        if i > 0:
            rw_wts["v0"] = row(rw_v0[i - 1])
            rw_wts["vup"] = jnp.pad(rw_v_up[i - 1], ((0, VD_COLS_PAD - RW_VRES_RANK), (0, 0))).astype(BF16)
        r, v, kn, lw, kd, ic, g, bn = _rw_prep(
            p_rw.reshape(B, S, RW_COLS), p_vd if i > 0 else None, v_first, rw_wts, ts=ts_["seq_ts"])
        if i == 0:
            v_first = v
        o2 = _rw_scan(r, v, kn, lw, kd, ic, tt=ts_["scan_tt"])

        ucv = _cv(p_cv.reshape(B, S, CV_COLS), cv_dw_w[i], row(cv_dw_b[i]), row(cv_ln_g[i]),
                  row(cv_ln_b[i]), ts=ts_["seq_ts"])

        wq = mla_w_uq[i].reshape(MLA_Q_RANK, MLA_HEADS, MLA_QK)
        wkv = mla_w_ukv[i].reshape(MLA_KV_RANK, MLA_HEADS, MLA_NOPE + MLA_V)
        gpad = lambda gvec: jnp.pad(gvec, (0, LANE - MLA_QK)).reshape(1, LANE)
        mla_wts = dict(
            q_norm=row(mla_q_norm[i]),
            wq=_head_block_cols(wq.reshape(MLA_Q_RANK, -1), MLA_QK).astype(BF16),
            kv_norm=row(mla_kv_norm[i]),
            wkv=jnp.concatenate([
                _head_block_cols(wkv[:, :, :MLA_NOPE].reshape(MLA_KV_RANK, -1), MLA_NOPE),
                _head_block_cols(wkv[:, :, MLA_NOPE:].reshape(MLA_KV_RANK, -1), MLA_V)], axis=1).astype(BF16),
            gq=gpad(mla_qk_q_g[i]), gk=gpad(mla_qk_k_g[i]))
        qh, kh, vh = _mla_prep(p_mla.reshape(B, S, MLA_COLS_PAD), tabs, mla_wts, ts=ts_["seq_ts"])
        omla = _attn(qh, kh, vh, tq=ts_["attn_tq"])

        m_wts = dict(e64=e64, gn_g=row(rw_gn_g[i]), gn_b=row(rw_gn_b[i]),
                     w_rw=rw_w_branch[i].astype(BF16), w_cv=cv_w_branch[i].astype(BF16),
                     w_mla=mla_w_branch[i].astype(BF16), w_o=w_o[i].astype(BF16))
        x2 = _merge(x2, o2.reshape(2, T, RW_WIDTH), g.reshape(T, -1), bn.reshape(T, -1), ucv.reshape(T, -1),
                    omla.reshape(T, -1), p_gate, m_wts, tm=ts_["tok_tm"])

        x2 = _ffn(x2, row(norm_ffn2[i]), ffn2_w1[i].astype(BF16), ffn2_w3[i].astype(BF16),
                  ffn2_w2[i].astype(BF16), tm=ts_["ffn_tm"], tf=ts_["ffn_tf"])
    return x2.reshape(B, S, D)
```

```python
import functools
import math

import jax
import jax.numpy as jnp
from jax import lax
from jax.experimental import pallas as pl
from jax.experimental.pallas import tpu as pltpu

F32 = jnp.float32
BF16 = jnp.bfloat16

D_MODEL = 1024
D_FF = 2816
NORM_EPS = 1e-6

RW_HEADS = 8
RW_HEAD_DIM = 64
RW_WIDTH = RW_HEADS * RW_HEAD_DIM
RW_DECAY_RANK = 64
RW_ICL_RANK = 64
RW_VRES_RANK = 32
RW_GATE_RANK = 128
RW_GN_EPS = 64e-5
RW_COLS = 3 * RW_WIDTH + 2 * RW_DECAY_RANK + 2 * RW_ICL_RANK + RW_GATE_RANK

CV_CHANNELS = 512
CONV_TAPS = 31
CV_LN_EPS = 1e-5
CV_COLS = 2 * CV_CHANNELS

MLA_HEADS = 8
MLA_Q_RANK = 384
MLA_KV_RANK = 256
MLA_NOPE = 64
MLA_ROPE = 32
MLA_QK = MLA_NOPE + MLA_ROPE
MLA_V = 64
ROPE_THETA = 10000.0
MLA_COLS = MLA_Q_RANK + MLA_KV_RANK + MLA_ROPE
MLA_COLS_PAD = 768
GATE_COLS = 3 * D_MODEL
VD_COLS_PAD = 128

LANE = 128
VMEM_LIMIT = 56 * 1024 * 1024

RW_CHUNK = 64


def _cparams(sem):
    return pltpu.CompilerParams(dimension_semantics=sem, vmem_limit_bytes=VMEM_LIMIT)


def _const_spec(shape):
    n = len(shape)
    return pl.BlockSpec(shape, lambda *_: (0,) * n)


def _mm(a, b):
    return jnp.dot(a.astype(BF16), b.astype(BF16), preferred_element_type=F32)


def _mm_nt(a, b):
    return lax.dot_general(a.astype(BF16), b.astype(BF16), (((1,), (1,)), ((), ())),
                           preferred_element_type=F32)


def _mm_tn(a, b):
    return lax.dot_general(a.astype(BF16), b.astype(BF16), (((0,), (0,)), ((), ())),
                           preferred_element_type=F32)


def _split3(x):
    hi = x.astype(BF16)
    r1 = x - hi.astype(F32)
    mid = r1.astype(BF16)
    lo = (r1 - mid.astype(F32)).astype(BF16)
    return hi, mid, lo


def _mm_x_exact(x, m):
    hi = x.astype(BF16)
    lo = (x - hi.astype(F32)).astype(BF16)
    return jnp.dot(hi, m, preferred_element_type=F32) + jnp.dot(lo, m, preferred_element_type=F32)


def _mm_exact_x(m, x):
    hi, mid, lo = _split3(x)
    return (jnp.dot(m, hi, preferred_element_type=F32) + jnp.dot(m, mid, preferred_element_type=F32)
            + jnp.dot(m, lo, preferred_element_type=F32))


def _sigmoid(x):
    return 1.0 / (1.0 + jnp.exp(-x))


def _rms(x, g, eps=NORM_EPS):
    return x * lax.rsqrt(jnp.mean(x * x, axis=-1, keepdims=True) + eps) * g


def _ffn_kernel(x_ref, g_ref, w1_ref, w3_ref, w2_ref, o_ref, h_ref):
    j = pl.program_id(1)

    @pl.when(j == 0)
    def _():
        x = x_ref[...]
        h_ref[...] = _rms(x, g_ref[...]).astype(BF16)
        o_ref[...] = x

    h = h_ref[...]
    a = jnp.dot(h, w1_ref[...], preferred_element_type=F32)
    b = jnp.dot(h, w3_ref[...], preferred_element_type=F32)
    u = (a * _sigmoid(a) * b).astype(BF16)
    o_ref[...] += 0.5 * jnp.dot(u, w2_ref[...], preferred_element_type=F32)


def _ffn(x2, g, w1, w3, w2, *, tm, tf):
    T, D = x2.shape
    F = w1.shape[1]
    return pl.pallas_call(
        _ffn_kernel,
        grid=(T // tm, F // tf),
        in_specs=[
            pl.BlockSpec((tm, D), lambda i, j: (i, 0)),
            pl.BlockSpec((1, D), lambda i, j: (0, 0)),
            pl.BlockSpec((D, tf), lambda i, j: (0, j)),
            pl.BlockSpec((D, tf), lambda i, j: (0, j)),
            pl.BlockSpec((tf, D), lambda i, j: (j, 0)),
        ],
        out_specs=pl.BlockSpec((tm, D), lambda i, j: (i, 0)),
        out_shape=jax.ShapeDtypeStruct((T, D), F32),
        scratch_shapes=[pltpu.VMEM((tm, D), BF16)],
        compiler_params=_cparams(("parallel", "arbitrary")),
        name="ffn",
    )(x2, g, w1, w3, w2)


def _inproj_kernel(x_ref, g_ref, w_ref, rw_ref, cv_ref, mla_ref, gate_ref, vd_ref):
    h = _rms(x_ref[...], g_ref[...]).astype(BF16)
    c = 0
    for ref in (rw_ref, cv_ref, mla_ref, gate_ref, vd_ref):
        n = ref.shape[1]
        ref[...] = jnp.dot(h, w_ref[:, c:c + n], preferred_element_type=F32).astype(ref.dtype)
        c += n


def _inproj(x2, g, w_cat, *, tm):
    T, D = x2.shape
    widths = (RW_COLS, CV_COLS, MLA_COLS_PAD, GATE_COLS, VD_COLS_PAD)
    assert w_cat.shape[1] == sum(widths)
    return pl.pallas_call(
        _inproj_kernel,
        grid=(T // tm,),
        in_specs=[
            pl.BlockSpec((tm, D), lambda i: (i, 0)),
            _const_spec((1, D)),
            _const_spec(w_cat.shape),
        ],
        out_specs=[pl.BlockSpec((tm, n), lambda i: (i, 0)) for n in widths],
        out_shape=[jax.ShapeDtypeStruct((T, n), F32) for n in widths],
        compiler_params=_cparams(("parallel",)),
        name="inproj",
    )(x2, g, w_cat)


def _rw_prep_kernel(*refs, has_vres):
    if has_vres:
        (p_ref, pp_ref, pn_ref, vd_ref, vf_ref, mu_ref, wup_ref, w0_ref, aup_ref, a0_ref, gup_ref,
         kk_ref, ka_ref, rk_ref, e_ref, v0_ref, vup_ref,
         r_o, v_o, kn_o, lw_o, kd_o, ic_o, g_o, bn_o) = refs
    else:
        (p_ref, pp_ref, pn_ref, mu_ref, wup_ref, w0_ref, aup_ref, a0_ref, gup_ref,
         kk_ref, ka_ref, rk_ref, e_ref,
         r_o, v_o, kn_o, lw_o, kd_o, ic_o, g_o, bn_o) = refs
    i = pl.program_id(1)
    n = pl.num_programs(1)
    C = RW_WIDTH
    p = p_ref[0]
    ts = p.shape[0]
    prev_row = jnp.where(i > 0, pp_ref[0, 7:8, :], 0.0)
    next_row = jnp.where(i < n - 1, pn_ref[0, 0:1, :], 0.0)
    rows = lax.broadcasted_iota(jnp.int32, (ts, 1), 0)
    prev = jnp.where(rows == 0, prev_row, pltpu.roll(p, 1, 0))
    nxt = jnp.where(rows == ts - 1, next_row, pltpu.roll(p, ts - 1, 0))
    pm = p + (0.5 * (prev + nxt) - p) * mu_ref[...]

    r = pm[:, 0:C]
    k = pm[:, C:2 * C]
    v = pm[:, 2 * C:3 * C]
    o1 = 3 * C
    wd = jnp.tanh(pm[:, o1:o1 + 2 * RW_DECAY_RANK])
    o2 = o1 + 2 * RW_DECAY_RANK
    ad = pm[:, o2:o2 + 2 * RW_ICL_RANK]
    o3 = o2 + 2 * RW_ICL_RANK
    gd = pm[:, o3:o3 + RW_GATE_RANK]

    wl = w0_ref[...] + _mm(wd, wup_ref[...])
    lw = (-math.exp(-0.5)) * _sigmoid(wl)
    icl = _sigmoid(a0_ref[...] + _mm(ad, aup_ref[...]))
    g = _mm(_sigmoid(gd), gup_ref[...])

    if has_vres:
        nu = _sigmoid(v0_ref[...] + _mm(vd_ref[...], vup_ref[...]))
        v = v + (vf_ref[0] - v) * nu

    e = e_ref[...]
    kkr = k * kk_ref[...]
    ss = _mm_x_exact(kkr * kkr, e)
    kn = kkr * lax.rsqrt(jnp.maximum(ss, 1e-24))

    ka = ka_ref[...]
    kd0 = k * (1.0 + (icl[:, 0:C] - 1.0) * ka)
    kd1 = k * (1.0 + (icl[:, C:2 * C] - 1.0) * ka)
    kb = 0.5 * (kd0 + kd1)
    sb = _mm_x_exact(r * kb * rk_ref[...], e)

    r_o[0] = r
    v_o[0] = v
    kn_o[0] = kn
    lw_o[0, 0] = lw[:, 0:C]
    lw_o[0, 1] = lw[:, C:2 * C]
    kd_o[0, 0] = kd0
    kd_o[0, 1] = kd1
    ic_o[0, 0] = icl[:, 0:C]
    ic_o[0, 1] = icl[:, C:2 * C]
    g_o[0] = g
    bn_o[0] = sb * v


def _rw_prep(p_rw, vd, v_first, wts, *, ts):
    B, S, _ = p_rw.shape
    C = RW_WIDTH
    has_vres = vd is not None
    nblk8 = S // 8
    tile = lambda b, i: (b, i, 0)
    in_arrays = [p_rw, p_rw, p_rw]
    in_specs = [
        pl.BlockSpec((1, ts, RW_COLS), tile),
        pl.BlockSpec((1, 8, RW_COLS), lambda b, i: (b, jnp.maximum(i * (ts // 8) - 1, 0), 0)),
        pl.BlockSpec((1, 8, RW_COLS), lambda b, i: (b, jnp.minimum((i + 1) * (ts // 8), nblk8 - 1), 0)),
    ]
    if has_vres:
        in_arrays += [vd, v_first]
        in_specs += [pl.BlockSpec((ts, VD_COLS_PAD), lambda b, i: (b * (S // ts) + i, 0)),
                     pl.BlockSpec((1, ts, C), tile)]
    names = ["mu", "wup", "w0", "aup", "a0", "gup", "k_k", "k_a", "r_k", "e64"]
    if has_vres:
        names += ["v0", "vup"]
    for nme in names:
        in_arrays.append(wts[nme])
        in_specs.append(_const_spec(wts[nme].shape))
    one = jax.ShapeDtypeStruct((B, S, C), F32)
    two = jax.ShapeDtypeStruct((B, 2, S, C), F32)
    ospec1 = pl.BlockSpec((1, ts, C), tile)
    ospec2 = pl.BlockSpec((1, 2, ts, C), lambda b, i: (b, 0, i, 0))
    return pl.pallas_call(
        functools.partial(_rw_prep_kernel, has_vres=has_vres),
        grid=(B, S // ts),
        in_specs=in_specs,
        out_specs=[ospec1, ospec1, ospec1, ospec2, ospec2, ospec2, ospec1, ospec1],
        out_shape=[one, one, one, two, two, two, one, one],
        compiler_params=_cparams(("parallel", "parallel")),
        name="rw_prep",
    )(*in_arrays)


def _rw_scan_kernel(rf_ref, vf_ref, knf_ref, lwf_ref, kdf_ref, icf_ref,
                    rb_ref, vb_ref, knb_ref, lwb_ref, kdb_ref, icb_ref,
                    of_ref, ob_ref, z_ref, *, n_chunks):
    j = pl.program_id(1)
    C = RW_CHUNK
    N = RW_HEAD_DIM
    H = RW_HEADS

    @pl.when(j == 0)
    def _():
        z_ref[...] = jnp.zeros_like(z_ref)

    tt_ = lax.broadcasted_iota(jnp.int32, (2 * C, 2 * C), 0)
    ii_ = lax.broadcasted_iota(jnp.int32, (2 * C, 2 * C), 1)
    order = (tt_ % C) - (ii_ % C)
    need = jnp.where(tt_ >= C, 0, 1)
    gmask = (order >= need, -order >= need)
    t1 = lax.broadcasted_iota(jnp.int32, (C, C), 0)
    i1 = lax.broadcasted_iota(jnp.int32, (C, C), 1)
    tri = (jnp.where(i1 <= t1, 1.0, 0.0).astype(BF16), jnp.where(i1 >= t1, 1.0, 0.0).astype(BF16))
    eye = jnp.where(i1 == t1, 1.0, 0.0)
    zeros_cn = jnp.zeros((C, N), BF16)
    in_refs = ((rf_ref, vf_ref, knf_ref, lwf_ref, kdf_ref, icf_ref),
               (rb_ref, vb_ref, knb_ref, lwb_ref, kdb_ref, icb_ref))
    out_refs = (of_ref, ob_ref)

    def chunk(cc, carry):
        rows_d = (pl.ds(pl.multiple_of(cc * C, C), C),
                  pl.ds(pl.multiple_of((n_chunks - 1 - cc) * C, C), C))
        wide = []
        for d in range(2):
            r_ref, v_ref, kn_ref, lw_ref, kd_ref, ic_ref = in_refs[d]
            rows = rows_d[d]
            r = r_ref[0, rows, :]
            v = v_ref[0, rows, :]
            kn = kn_ref[0, rows, :]
            lw = lw_ref[0, 0, rows, :]
            kd = kd_ref[0, 0, rows, :]
            ic = ic_ref[0, 0, rows, :]
            lc = _mm_exact_x(tri[d], lw)
            ltot = jnp.sum(lw, axis=0, keepdims=True)
            e_nlc = jnp.exp(-lc)
            e_t = jnp.exp(ltot - lc)
            b = kn * ic
            wide.append(dict(
                at=(-kn * jnp.exp(lc - lw)).astype(BF16), rt=(r * jnp.exp(lc)).astype(BF16),
                bt=(b * e_nlc).astype(BF16), kt=(kd * e_nlc).astype(BF16),
                bh=(b * e_t).astype(BF16), kh=(kd * e_t).astype(BF16),
                v=v.astype(BF16), gam=jnp.exp(ltot)))
        chains = [(d, h) for d in range(2) for h in range(H)]
        hs = lambda d, h, name: wide[d][name][:, h * N:(h + 1) * N]

        g = [jnp.where(gmask[d], _mm_nt(jnp.concatenate([hs(d, h, "at"), hs(d, h, "rt")], axis=0),
                                        jnp.concatenate([hs(d, h, "bt"), hs(d, h, "kt")], axis=0)), 0.0)
             for d, h in chains]
        l1 = [gi[:C, :C] for gi in g]
        tinv = [eye + l for l in l1]
        lp = [_mm(l, l) for l in l1]
        n_lvl = int(math.log2(C))
        for lvl in range(2, n_lvl):
            both = [_mm(jnp.concatenate([l, t], axis=0), l) for l, t in zip(lp, tinv)]
            tinv = [t + bo[C:] for t, bo in zip(tinv, both)]
            lp = [bo[:C] for bo in both]
        tinv = [t + _mm(t, l) for t, l in zip(tinv, lp)]
        ah = [_mm(t, hs(d, h, "at")) for t, (d, h) in zip(tinv, chains)]
        w = [_mm(gi[:C], jnp.concatenate([zeros_cn, hs(d, h, "v")], axis=0)) for gi, (d, h) in zip(g, chains)]
        vh = [_mm(t, wi) for t, wi in zip(tinv, w)]
        z = [z_ref[n] for n in range(2 * H)]
        az = [_mm_nt(jnp.concatenate([a.astype(BF16), hs(d, h, "rt")], axis=0), zi)
              for a, zi, (d, h) in zip(ah, z, chains)]
        uv = [jnp.concatenate([(azi[:C] + vhi).astype(BF16), hs(d, h, "v")], axis=0)
              for azi, vhi, (d, h) in zip(az, vh, chains)]
        o = [azi[C:] + _mm(gi[C:], uvi) for azi, gi, uvi in zip(az, g, uv)]
        znew = [zi * wide[d]["gam"][:, h * N:(h + 1) * N]
                + _mm_tn(uvi, jnp.concatenate([hs(d, h, "bh"), hs(d, h, "kh")], axis=0))
                for zi, uvi, (d, h) in zip(z, uv, chains)]
        z_ref[...] = jnp.stack(znew, axis=0)
        for d in range(2):
            out_refs[d][0, rows_d[d], :] = jnp.concatenate(o[d * H:(d + 1) * H], axis=1)
        return carry

    lax.fori_loop(0, n_chunks, chunk, 0)


def _rw_scan(r, v, kn, lw, kd, ic, *, tt):
    B, S, C = r.shape
    nT = S // tt
    f1 = pl.BlockSpec((1, tt, C), lambda b, j: (b, j, 0))
    b1 = pl.BlockSpec((1, tt, C), lambda b, j: (b, nT - 1 - j, 0))
    f2 = pl.BlockSpec((1, 1, tt, C), lambda b, j: (b, 0, j, 0))
    b2 = pl.BlockSpec((1, 1, tt, C), lambda b, j: (b, 1, nT - 1 - j, 0))
    out = jax.ShapeDtypeStruct((B, S, C), F32)
    return pl.pallas_call(
        functools.partial(_rw_scan_kernel, n_chunks=tt // RW_CHUNK),
        grid=(B, nT),
        in_specs=[f1, f1, f1, f2, f2, f2, b1, b1, b1, b2, b2, b2],
        out_specs=[f1, b1],
        out_shape=[out, out],
        scratch_shapes=[pltpu.VMEM((2 * RW_HEADS, RW_HEAD_DIM, RW_HEAD_DIM), F32)],
        compiler_params=_cparams(("parallel", "arbitrary")),
        name="rw_scan",
    )(r, v, kn, lw, kd, ic, r, v, kn, lw, kd, ic)


CV_HALO = 16


def _cv_kernel(p_ref, pp_ref, pn_ref, dw_ref, db_ref, lg_ref, lb_ref, o_ref, u_ref):
    i = pl.program_id(1)
    n = pl.num_programs(1)
    ts = p_ref.shape[1]
    Cc = CV_CHANNELS

    def glu(x):
        return x[:, :Cc] * _sigmoid(x[:, Cc:])

    u_ref[CV_HALO:CV_HALO + ts, :] = glu(p_ref[0])
    u_ref[0:CV_HALO, :] = jnp.where(i > 0, glu(pp_ref[0]), 0.0)
    u_ref[CV_HALO + ts:, :] = jnp.where(i < n - 1, glu(pn_ref[0]), 0.0)
    base = CV_HALO - CONV_TAPS // 2
    ext = ts + 8
    acc = None
    for res in range(8):
        part = None
        for s in range(res, base + CONV_TAPS, 8):
            if s < base:
                continue
            term = u_ref[s - res:s - res + ext, :] * dw_ref[s - base:s - base + 1, :]
            part = term if part is None else part + term
        if part is None:
            continue
        shifted = part[:ts] if res == 0 else pltpu.roll(part, ext - res, 0)[:ts]
        acc = shifted if acc is None else acc + shifted
    acc = acc + db_ref[...]
    mu = jnp.mean(acc, axis=-1, keepdims=True)
    xc = acc - mu
    var = jnp.mean(xc * xc, axis=-1, keepdims=True)
    y = xc * lax.rsqrt(var + CV_LN_EPS) * lg_ref[...] + lb_ref[...]
    o_ref[0] = (y * _sigmoid(y)).astype(o_ref.dtype)


def _cv(p_cv, dw, db, lg, lb, *, ts):
    B, S, _ = p_cv.shape
    nh = S // CV_HALO
    return pl.pallas_call(
        _cv_kernel,
        grid=(B, S // ts),
        in_specs=[
            pl.BlockSpec((1, ts, CV_COLS), lambda b, i: (b, i, 0)),
            pl.BlockSpec((1, CV_HALO, CV_COLS),
                         lambda b, i: (b, jnp.maximum(i * (ts // CV_HALO) - 1, 0), 0)),
            pl.BlockSpec((1, CV_HALO, CV_COLS),
                         lambda b, i: (b, jnp.minimum((i + 1) * (ts // CV_HALO), nh - 1), 0)),
            _const_spec(dw.shape), _const_spec(db.shape), _const_spec(lg.shape), _const_spec(lb.shape),
        ],
        out_specs=pl.BlockSpec((1, ts, CV_CHANNELS), lambda b, i: (b, i, 0)),
        out_shape=jax.ShapeDtypeStruct((B, S, CV_CHANNELS), BF16),
        scratch_shapes=[pltpu.VMEM((ts + 2 * CV_HALO, CV_CHANNELS), F32)],
        compiler_params=_cparams(("parallel", "parallel")),
        name="cv_branch",
    )(p_cv, p_cv, p_cv, dw, db, lg, lb)


def _mla_prep_kernel(p_ref, cb_ref, s1_ref, s2_ref, qn_ref, wq_ref, wqs_ref, kvn_ref, wkv_ref,
                     gq_ref, gqs_ref, gk_ref, q_o, k_o, v_o):
    p = p_ref[0]
    cb = cb_ref[0]
    s1 = s1_ref[0]
    s2 = s2_ref[0]
    s12 = s1 + s2
    cq = p[:, 0:MLA_Q_RANK]
    ckv = p[:, MLA_Q_RANK:MLA_Q_RANK + MLA_KV_RANK]
    kr = p[:, MLA_Q_RANK + MLA_KV_RANK:MLA_COLS_PAD]
    cqn = _rms(cq, qn_ref[...]).astype(BF16)
    qall = jnp.dot(cqn, wq_ref[...], preferred_element_type=F32)
    qswp = jnp.dot(cqn, wqs_ref[...], preferred_element_type=F32)
    kvall = _mm(_rms(ckv, kvn_ref[...]), wkv_ref[...])
    gq = gq_ref[...]
    gqs = gqs_ref[...]
    gk = gk_ref[...]
    lane = lax.broadcasted_iota(jnp.int32, (1, LANE), 1)
    ones_col = jnp.where(lane == MLA_V, 1.0, 0.0)
    scale = MLA_QK ** -0.5
    HB = MLA_HEADS * LANE

    krot = pltpu.roll(kr, MLA_NOPE, 1)
    kr_ss = jnp.sum(krot * krot, axis=-1, keepdims=True)
    krg = krot * gk
    kr_r = (krg * cb + pltpu.roll(krg, LANE - MLA_ROPE // 2, 1) * s1
            + pltpu.roll(krg, MLA_ROPE // 2, 1) * s2)

    for h in range(MLA_HEADS):
        blk = slice(h * LANE, (h + 1) * LANE)
        qb = qall[:, blk]
        rq = lax.rsqrt(jnp.sum(qb * qb, axis=-1, keepdims=True) * (1.0 / MLA_QK) + NORM_EPS)
        q_o[0, h] = (((qb * gq) * cb + (qswp[:, blk] * gqs) * s12) * (rq * scale)).astype(q_o.dtype)
        kb = kvall[:, blk]
        rk = lax.rsqrt((jnp.sum(kb * kb, axis=-1, keepdims=True) + kr_ss) * (1.0 / MLA_QK) + NORM_EPS)
        k_o[0, h] = ((kb * gk + kr_r) * rk).astype(k_o.dtype)
        v_o[0, h] = (kvall[:, HB + h * LANE:HB + (h + 1) * LANE] + ones_col).astype(v_o.dtype)


def _mla_prep(p_mla, tabs, wts, *, ts):
    B, S, _ = p_mla.shape
    H = MLA_HEADS
    tile = lambda b, i: (b, i, 0)
    tspec = pl.BlockSpec((1, ts, LANE), tile)
    names = ["q_norm", "wq", "wq_swap", "kv_norm", "wkv", "gq", "gq_swap", "gk"]
    hspec = pl.BlockSpec((1, H, ts, LANE), lambda b, i: (b, 0, i, 0))
    hshape = jax.ShapeDtypeStruct((B, H, S, LANE), BF16)
    return pl.pallas_call(
        _mla_prep_kernel,
        grid=(B, S // ts),
        in_specs=[pl.BlockSpec((1, ts, MLA_COLS_PAD), tile), tspec, tspec, tspec]
        + [_const_spec(wts[n].shape) for n in names],
        out_specs=[hspec, hspec, hspec],
        out_shape=[hshape, hshape, hshape],
        compiler_params=_cparams(("parallel", "parallel")),
        name="mla_prep",
    )(p_mla, *tabs, *[wts[n] for n in names])


ATTN_SUB = 256


def _attn_kernel(q_ref, k_ref, v_ref, o_ref):
    tq = q_ref.shape[2]
    for r0 in range(0, tq, ATTN_SUB):
        outs = []
        for h in range(2):
            s = _mm_nt(q_ref[0, h, r0:r0 + ATTN_SUB, :], k_ref[0, h])
            m = jnp.max(s, axis=-1, keepdims=True)
            p = jnp.exp(s - m).astype(BF16)
            ov = jnp.dot(p, v_ref[0, h], preferred_element_type=F32)
            outs.append(ov[:, :MLA_V] / ov[:, MLA_V:MLA_V + 1])
        o_ref[0, r0:r0 + ATTN_SUB, :] = jnp.concatenate(outs, axis=1).astype(o_ref.dtype)


def _attn(q, k, v, *, tq):
    B, H, S, _ = q.shape
    return pl.pallas_call(
        _attn_kernel,
        grid=(B, H // 2, S // tq),
        in_specs=[
            pl.BlockSpec((1, 2, tq, LANE), lambda b, h, i: (b, h, i, 0)),
            pl.BlockSpec((1, 2, S, LANE), lambda b, h, i: (b, h, 0, 0)),
            pl.BlockSpec((1, 2, S, LANE), lambda b, h, i: (b, h, 0, 0)),
        ],
        out_specs=pl.BlockSpec((1, tq, 2 * MLA_V), lambda b, h, i: (b, i, h)),
        out_shape=jax.ShapeDtypeStruct((B, S, H * MLA_V), BF16),
        compiler_params=_cparams(("parallel", "parallel", "parallel")),
        name="mla_attn",
    )(q, k, v)


def _merge_kernel(x_ref, of_ref, ob_ref, g_ref, bn_ref, ucv_ref, omla_ref, gate_ref,
                  avg_ref, gng_ref, gnb_ref, wrw_ref, wcv_ref, wmla_ref, wo_ref, out_ref):
    D = D_MODEL
    o = of_ref[...] + ob_ref[...]
    avg = avg_ref[...]
    inv_n = 1.0 / RW_HEAD_DIM
    mu = _mm_x_exact(o, avg) * inv_n
    xc = o - mu
    var = _mm_x_exact(xc * xc, avg) * inv_n
    on = xc * lax.rsqrt(var + RW_GN_EPS) * gng_ref[...] + gnb_ref[...]
    orw = (on + bn_ref[...]) * g_ref[...]
    gate = gate_ref[...]
    merged = (_sigmoid(gate[:, 0:D]) * _mm(orw, wrw_ref[...])
              + _sigmoid(gate[:, D:2 * D]) * jnp.dot(ucv_ref[...], wcv_ref[...], preferred_element_type=F32)
              + _sigmoid(gate[:, 2 * D:3 * D]) * jnp.dot(omla_ref[...], wmla_ref[...],
                                                         preferred_element_type=F32))
    out_ref[...] = x_ref[...] + _mm(merged, wo_ref[...])


def _merge(x2, o_f, o_b, g, bn, ucv, omla, gate, wts, *, tm):
    T, D = x2.shape
    C = RW_WIDTH
    row = lambda n: pl.BlockSpec((tm, n), lambda i: (i, 0))
    names = ["e64", "gn_g", "gn_b", "w_rw", "w_cv", "w_mla", "w_o"]
    return pl.pallas_call(
        _merge_kernel,
        grid=(T // tm,),
        in_specs=[row(D), row(C), row(C), row(C), row(C), row(C), row(C),
                  row(GATE_COLS)] + [_const_spec(wts[n].shape) for n in names],
        out_specs=row(D),
        out_shape=jax.ShapeDtypeStruct((T, D), F32),
        compiler_params=_cparams(("parallel",)),
        name="merge",
    )(x2, o_f, o_b, g, bn, ucv, omla, gate, *[wts[n] for n in names])


def _block_diag2(w):
    z = jnp.zeros_like(w[0])
    return jnp.concatenate([jnp.concatenate([w[0], z], axis=1), jnp.concatenate([z, w[1]], axis=1)], axis=0)


def _head_block_cols(w, width):
    R = w.shape[0]
    w = w.reshape(R, MLA_HEADS, width)
    return jnp.pad(w, ((0, 0), (0, 0), (0, LANE - width))).reshape(R, MLA_HEADS * LANE)


def _rope_tables(positions):
    half = MLA_ROPE // 2
    inv = ROPE_THETA ** (-jnp.arange(0, MLA_ROPE, 2, dtype=F32) / MLA_ROPE)
    ang = positions.astype(F32)[..., None] * inv
    cos, sin = jnp.cos(ang), jnp.sin(ang)
    B, S = positions.shape
    one = jnp.ones((B, S, MLA_NOPE), F32)
    z64 = jnp.zeros((B, S, MLA_NOPE), F32)
    z16 = jnp.zeros((B, S, half), F32)
    pad = jnp.zeros((B, S, LANE - MLA_QK), F32)
    cb = jnp.concatenate([one, cos, cos, pad + 1.0], axis=-1)
    s1 = jnp.concatenate([z64, -sin, z16, pad], axis=-1)
    s2 = jnp.concatenate([z64, z16, sin, pad], axis=-1)
    return cb, s1, s2


def _head_ones():
    hid = jnp.arange(RW_WIDTH) // RW_HEAD_DIM
    return (hid[:, None] == hid[None, :]).astype(BF16)


def _tile_sizes(T, S):
    pick = lambda n, cands: next(c for c in cands if n % c == 0)
    return dict(
        ffn_tm=pick(T, (2048, 1024, 512, 256, 128, 64, 32, 16, 8)),
        ffn_tf=pick(D_FF, (256, 128)),
        tok_tm=pick(T, (512, 256, 128, 64, 32, 16, 8)),
        seq_ts=pick(S, (512, 256, 128, 64)),
        scan_tt=pick(S, (512, 256, 128, 64)),
        attn_tq=pick(S, (1024, 512, 256)),
    )


def kernel(x, positions, norm_ffn1, ffn1_w1, ffn1_w3, ffn1_w2, norm_mix, w_in, rw_mu, rw_w0, rw_w_up, rw_a0, rw_a_up, rw_g_up, rw_k_k, rw_k_a, rw_r_k, rw_v0, rw_v_down, rw_v_up, rw_gn_g, rw_gn_b, rw_w_branch, cv_dw_w, cv_dw_b, cv_ln_g, cv_ln_b, cv_w_branch, mla_q_norm, mla_w_uq, mla_kv_norm, mla_w_ukv, mla_qk_q_g, mla_qk_k_g, mla_w_branch, w_o, norm_ffn2, ffn2_w1, ffn2_w3, ffn2_w2):
    B, S, D = x.shape
    T = B * S
    L = w_in.shape[0]
    ts_ = _tile_sizes(T, S)
    tabs = _rope_tables(positions)
    e64 = _head_ones()
    row = lambda a: a.reshape(1, -1)
    x2 = x.reshape(T, D)
    v_first = None
    for i in range(L):
        x2 = _ffn(x2, row(norm_ffn1[i]), ffn1_w1[i].astype(BF16), ffn1_w3[i].astype(BF16),
                  ffn1_w2[i].astype(BF16), tm=ts_["ffn_tm"], tf=ts_["ffn_tf"])

        c1, c2, c3 = RW_COLS, RW_COLS + CV_COLS, RW_COLS + CV_COLS + MLA_COLS
        wi = w_in[i]
        vdw = rw_v_down[i - 1] if i > 0 else jnp.zeros((D, RW_VRES_RANK), F32)
        w_cat = jnp.concatenate([
            wi[:, :c1], wi[:, c1:c2],
            jnp.pad(wi[:, c2:c3], ((0, 0), (0, MLA_COLS_PAD - MLA_COLS))),
            wi[:, c3:], jnp.pad(vdw, ((0, 0), (0, VD_COLS_PAD - RW_VRES_RANK)))], axis=1).astype(BF16)
        p_rw, p_cv, p_mla, p_gate, p_vd = _inproj(x2, row(norm_mix[i]), w_cat, tm=ts_["tok_tm"])

        rw_wts = dict(
            mu=row(rw_mu[i]), wup=_block_diag2(rw_w_up[i]).astype(BF16), w0=row(rw_w0[i]),
            aup=_block_diag2(rw_a_up[i]).astype(BF16), a0=row(rw_a0[i]), gup=rw_g_up[i].astype(BF16),
            k_k=row(rw_k_k[i]), k_a=row(rw_k_a[i]), r_k=row(rw_r_k[i]), e64=e64)
        if i > 0:
            rw_wts["v0"] = row(rw_v0[i - 1])
            rw_wts["vup"] = jnp.pad(rw_v_up[i - 1], ((0, VD_COLS_PAD - RW_VRES_RANK), (0, 0))).astype(BF16)
        r, v, kn, lw, kd, ic, g, bn = _rw_prep(
            p_rw.reshape(B, S, RW_COLS), p_vd if i > 0 else None, v_first, rw_wts, ts=ts_["seq_ts"])
        if i == 0:
            v_first = v
        o_f, o_b = _rw_scan(r, v, kn, lw, kd, ic, tt=ts_["scan_tt"])

        ucv = _cv(p_cv.reshape(B, S, CV_COLS), cv_dw_w[i], row(cv_dw_b[i]), row(cv_ln_g[i]),
                  row(cv_ln_b[i]), ts=ts_["seq_ts"])

        wq = mla_w_uq[i].reshape(MLA_Q_RANK, MLA_HEADS, MLA_QK)
        wkv = mla_w_ukv[i].reshape(MLA_KV_RANK, MLA_HEADS, MLA_NOPE + MLA_V)
        gpad = lambda gvec: jnp.pad(gvec, (0, LANE - MLA_QK)).reshape(1, LANE)
        half = MLA_ROPE // 2
        swap = lambda a: jnp.concatenate(
            [a[..., :MLA_NOPE], a[..., MLA_NOPE + half:], a[..., MLA_NOPE:MLA_NOPE + half]], axis=-1)
        mla_wts = dict(
            q_norm=row(mla_q_norm[i]),
            wq=_head_block_cols(wq.reshape(MLA_Q_RANK, -1), MLA_QK).astype(BF16),
            wq_swap=_head_block_cols(swap(wq).reshape(MLA_Q_RANK, -1), MLA_QK).astype(BF16),
            gq_swap=gpad(swap(mla_qk_q_g[i])),
            kv_norm=row(mla_kv_norm[i]),
            wkv=jnp.concatenate([
                _head_block_cols(wkv[:, :, :MLA_NOPE].reshape(MLA_KV_RANK, -1), MLA_NOPE),
                _head_block_cols(wkv[:, :, MLA_NOPE:].reshape(MLA_KV_RANK, -1), MLA_V)], axis=1).astype(BF16),
            gq=gpad(mla_qk_q_g[i]), gk=gpad(mla_qk_k_g[i]))
        qh, kh, vh = _mla_prep(p_mla.reshape(B, S, MLA_COLS_PAD), tabs, mla_wts, ts=ts_["seq_ts"])
        omla = _attn(qh, kh, vh, tq=ts_["attn_tq"])

        m_wts = dict(e64=e64, gn_g=row(rw_gn_g[i]), gn_b=row(rw_gn_b[i]),
                     w_rw=rw_w_branch[i].astype(BF16), w_cv=cv_w_branch[i].astype(BF16),
                     w_mla=mla_w_branch[i].astype(BF16), w_o=w_o[i].astype(BF16))
        x2 = _merge(x2, o_f.reshape(T, -1), o_b.reshape(T, -1), g.reshape(T, -1), bn.reshape(T, -1), ucv.reshape(T, -1),
                    omla.reshape(T, -1), p_gate, m_wts, tm=ts_["tok_tm"])

        x2 = _ffn(x2, row(norm_ffn2[i]), ffn2_w1[i].astype(BF16), ffn2_w3[i].astype(BF16),
                  ffn2_w2[i].astype(BF16), tm=ts_["ffn_tm"], tf=ts_["ffn_tf"])
    return x2.reshape(B, S, D)
```

```python
import functools
import math

import jax
import jax.numpy as jnp
from jax import lax
from jax.experimental import pallas as pl
from jax.experimental.pallas import tpu as pltpu

F32 = jnp.float32
BF16 = jnp.bfloat16

D_MODEL = 1024
D_FF = 2816
NORM_EPS = 1e-6

RW_HEADS = 8
RW_HEAD_DIM = 64
RW_WIDTH = RW_HEADS * RW_HEAD_DIM
RW_DECAY_RANK = 64
RW_ICL_RANK = 64
RW_VRES_RANK = 32
RW_GATE_RANK = 128
RW_GN_EPS = 64e-5
RW_COLS = 3 * RW_WIDTH + 2 * RW_DECAY_RANK + 2 * RW_ICL_RANK + RW_GATE_RANK

CV_CHANNELS = 512
CONV_TAPS = 31
CV_LN_EPS = 1e-5
CV_COLS = 2 * CV_CHANNELS

MLA_HEADS = 8
MLA_Q_RANK = 384
MLA_KV_RANK = 256
MLA_NOPE = 64
MLA_ROPE = 32
MLA_QK = MLA_NOPE + MLA_ROPE
MLA_V = 64
ROPE_THETA = 10000.0
MLA_COLS = MLA_Q_RANK + MLA_KV_RANK + MLA_ROPE
MLA_COLS_PAD = 768
GATE_COLS = 3 * D_MODEL
VD_COLS_PAD = 128

LANE = 128
VMEM_LIMIT = 56 * 1024 * 1024

RW_CHUNK = 64


def _cparams(sem):
    return pltpu.CompilerParams(dimension_semantics=sem, vmem_limit_bytes=VMEM_LIMIT)


def _const_spec(shape):
    n = len(shape)
    return pl.BlockSpec(shape, lambda *_: (0,) * n)


def _mm(a, b):
    return jnp.dot(a.astype(BF16), b.astype(BF16), preferred_element_type=F32)


def _mm_nt(a, b):
    return lax.dot_general(a.astype(BF16), b.astype(BF16), (((1,), (1,)), ((), ())),
                           preferred_element_type=F32)


def _mm_tn(a, b):
    return lax.dot_general(a.astype(BF16), b.astype(BF16), (((0,), (0,)), ((), ())),
                           preferred_element_type=F32)


def _split3(x):
    hi = x.astype(BF16)
    r1 = x - hi.astype(F32)
    mid = r1.astype(BF16)
    lo = (r1 - mid.astype(F32)).astype(BF16)
    return hi, mid, lo


def _mm_x_exact(x, m):
    hi = x.astype(BF16)
    lo = (x - hi.astype(F32)).astype(BF16)
    return jnp.dot(hi, m, preferred_element_type=F32) + jnp.dot(lo, m, preferred_element_type=F32)


def _mm_exact_x(m, x):
    hi, mid, lo = _split3(x)
    return (jnp.dot(m, hi, preferred_element_type=F32) + jnp.dot(m, mid, preferred_element_type=F32)
            + jnp.dot(m, lo, preferred_element_type=F32))


def _sigmoid(x):
    return 1.0 / (1.0 + jnp.exp(-x))


def _rms(x, g, eps=NORM_EPS):
    return x * lax.rsqrt(jnp.mean(x * x, axis=-1, keepdims=True) + eps) * g


def _ffn_kernel(x_ref, g_ref, w1_ref, w3_ref, w2_ref, o_ref, h_ref):
    j = pl.program_id(1)

    @pl.when(j == 0)
    def _():
        x = x_ref[...]
        h_ref[...] = _rms(x, g_ref[...]).astype(BF16)
        o_ref[...] = x

    h = h_ref[...]
    a = jnp.dot(h, w1_ref[...], preferred_element_type=F32)
    b = jnp.dot(h, w3_ref[...], preferred_element_type=F32)
    u = (a * _sigmoid(a) * b).astype(BF16)
    o_ref[...] += 0.5 * jnp.dot(u, w2_ref[...], preferred_element_type=F32)


def _ffn(x2, g, w1, w3, w2, *, tm, tf):
    T, D = x2.shape
    F = w1.shape[1]
    return pl.pallas_call(
        _ffn_kernel,
        grid=(T // tm, F // tf),
        in_specs=[
            pl.BlockSpec((tm, D), lambda i, j: (i, 0)),
            pl.BlockSpec((1, D), lambda i, j: (0, 0)),
            pl.BlockSpec((D, tf), lambda i, j: (0, j)),
            pl.BlockSpec((D, tf), lambda i, j: (0, j)),
            pl.BlockSpec((tf, D), lambda i, j: (j, 0)),
        ],
        out_specs=pl.BlockSpec((tm, D), lambda i, j: (i, 0)),
        out_shape=jax.ShapeDtypeStruct((T, D), F32),
        scratch_shapes=[pltpu.VMEM((tm, D), BF16)],
        compiler_params=_cparams(("parallel", "arbitrary")),
        name="ffn",
    )(x2, g, w1, w3, w2)


def _inproj_kernel(x_ref, g_ref, w_ref, rw_ref, cv_ref, mla_ref, gate_ref, vd_ref):
    h = _rms(x_ref[...], g_ref[...]).astype(BF16)
    c = 0
    for ref in (rw_ref, cv_ref, mla_ref, gate_ref, vd_ref):
        n = ref.shape[1]
        ref[...] = jnp.dot(h, w_ref[:, c:c + n], preferred_element_type=F32).astype(ref.dtype)
        c += n


def _inproj(x2, g, w_cat, *, tm):
    T, D = x2.shape
    widths = (RW_COLS, CV_COLS, MLA_COLS_PAD, GATE_COLS, VD_COLS_PAD)
    assert w_cat.shape[1] == sum(widths)
    return pl.pallas_call(
        _inproj_kernel,
        grid=(T // tm,),
        in_specs=[
            pl.BlockSpec((tm, D), lambda i: (i, 0)),
            _const_spec((1, D)),
            _const_spec(w_cat.shape),
        ],
        out_specs=[pl.BlockSpec((tm, n), lambda i: (i, 0)) for n in widths],
        out_shape=[jax.ShapeDtypeStruct((T, n), dt)
                   for n, dt in zip(widths, (F32, BF16, BF16, BF16, F32))],
        compiler_params=_cparams(("parallel",)),
        name="inproj",
    )(x2, g, w_cat)


def _rw_prep_kernel(*refs, has_vres):
    if has_vres:
        (p_ref, pp_ref, pn_ref, vd_ref, vf_ref, mu_ref, wup_ref, w0_ref, aup_ref, a0_ref, gup_ref,
         kk_ref, ka_ref, rk_ref, e_ref, v0_ref, vup_ref,
         r_o, v_o, kn_o, lw_o, kd_o, ic_o, g_o, bn_o) = refs
    else:
        (p_ref, pp_ref, pn_ref, mu_ref, wup_ref, w0_ref, aup_ref, a0_ref, gup_ref,
         kk_ref, ka_ref, rk_ref, e_ref,
         r_o, v_o, kn_o, lw_o, kd_o, ic_o, g_o, bn_o) = refs
    i = pl.program_id(1)
    n = pl.num_programs(1)
    C = RW_WIDTH
    p = p_ref[0]
    ts = p.shape[0]
    prev_row = jnp.where(i > 0, pp_ref[0, 7:8, :], 0.0)
    next_row = jnp.where(i < n - 1, pn_ref[0, 0:1, :], 0.0)
    rows = lax.broadcasted_iota(jnp.int32, (ts, 1), 0)
    prev = jnp.where(rows == 0, prev_row, pltpu.roll(p, 1, 0))
    nxt = jnp.where(rows == ts - 1, next_row, pltpu.roll(p, ts - 1, 0))
    pm = p + (0.5 * (prev + nxt) - p) * mu_ref[...]

    r = pm[:, 0:C]
    k = pm[:, C:2 * C]
    v = pm[:, 2 * C:3 * C]
    o1 = 3 * C
    wd = jnp.tanh(pm[:, o1:o1 + 2 * RW_DECAY_RANK])
    o2 = o1 + 2 * RW_DECAY_RANK
    ad = pm[:, o2:o2 + 2 * RW_ICL_RANK]
    o3 = o2 + 2 * RW_ICL_RANK
    gd = pm[:, o3:o3 + RW_GATE_RANK]

    wl = w0_ref[...] + _mm(wd, wup_ref[...])
    lw = (-math.exp(-0.5)) * _sigmoid(wl)
    icl = _sigmoid(a0_ref[...] + _mm(ad, aup_ref[...]))
    g = _mm(_sigmoid(gd), gup_ref[...])

    if has_vres:
        nu = _sigmoid(v0_ref[...] + _mm(vd_ref[...], vup_ref[...]))
        v = v + (vf_ref[0] - v) * nu

    e = e_ref[...]
    kkr = k * kk_ref[...]
    ss = _mm_x_exact(kkr * kkr, e)
    kn = kkr * lax.rsqrt(jnp.maximum(ss, 1e-24))

    ka = ka_ref[...]
    kd0 = k * (1.0 + (icl[:, 0:C] - 1.0) * ka)
    kd1 = k * (1.0 + (icl[:, C:2 * C] - 1.0) * ka)
    kb = 0.5 * (kd0 + kd1)
    sb = _mm_x_exact(r * kb * rk_ref[...], e)

    r_o[0] = r.astype(r_o.dtype)
    v_o[0] = v.astype(v_o.dtype)
    kn_o[0] = kn.astype(kn_o.dtype)
    lw_o[0, 0] = lw[:, 0:C]
    lw_o[0, 1] = lw[:, C:2 * C]
    kd_o[0, 0] = kd0.astype(kd_o.dtype)
    kd_o[0, 1] = kd1.astype(kd_o.dtype)
    ic_o[0, 0] = (kn * icl[:, 0:C]).astype(ic_o.dtype)
    ic_o[0, 1] = (kn * icl[:, C:2 * C]).astype(ic_o.dtype)
    g_o[0] = g.astype(g_o.dtype)
    bn_o[0] = (sb * v).astype(bn_o.dtype)


def _rw_prep(p_rw, vd, v_first, wts, *, ts):
    B, S, _ = p_rw.shape
    C = RW_WIDTH
    has_vres = vd is not None
    nblk8 = S // 8
    tile = lambda b, i: (b, i, 0)
    in_arrays = [p_rw, p_rw, p_rw]
    in_specs = [
        pl.BlockSpec((1, ts, RW_COLS), tile),
        pl.BlockSpec((1, 8, RW_COLS), lambda b, i: (b, jnp.maximum(i * (ts // 8) - 1, 0), 0)),
        pl.BlockSpec((1, 8, RW_COLS), lambda b, i: (b, jnp.minimum((i + 1) * (ts // 8), nblk8 - 1), 0)),
    ]
    if has_vres:
        in_arrays += [vd, v_first]
        in_specs += [pl.BlockSpec((ts, VD_COLS_PAD), lambda b, i: (b * (S // ts) + i, 0)),
                     pl.BlockSpec((1, ts, C), tile)]
    names = ["mu", "wup", "w0", "aup", "a0", "gup", "k_k", "k_a", "r_k", "e64"]
    if has_vres:
        names += ["v0", "vup"]
    for nme in names:
        in_arrays.append(wts[nme])
        in_specs.append(_const_spec(wts[nme].shape))
    one = jax.ShapeDtypeStruct((B, S, C), BF16)
    two = jax.ShapeDtypeStruct((B, 2, S, C), BF16)
    two_f32 = jax.ShapeDtypeStruct((B, 2, S, C), F32)
    ospec1 = pl.BlockSpec((1, ts, C), tile)
    ospec2 = pl.BlockSpec((1, 2, ts, C), lambda b, i: (b, 0, i, 0))
    return pl.pallas_call(
        functools.partial(_rw_prep_kernel, has_vres=has_vres),
        grid=(B, S // ts),
        in_specs=in_specs,
        out_specs=[ospec1, ospec1, ospec1, ospec2, ospec2, ospec2, ospec1, ospec1],
        out_shape=[one, one, one, two_f32, two, two, one, one],
        compiler_params=_cparams(("parallel", "parallel")),
        name="rw_prep",
    )(*in_arrays)


def _rw_scan_kernel(rf_ref, vf_ref, knf_ref, lwf_ref, kdf_ref, bbf_ref,
                    rb_ref, vb_ref, knb_ref, lwb_ref, kdb_ref, bbb_ref,
                    of_ref, ob_ref, z_ref, *, n_chunks):
    j = pl.program_id(1)
    C = RW_CHUNK
    N = RW_HEAD_DIM
    P = RW_HEADS // 2
    W = 2 * N

    @pl.when(j == 0)
    def _():
        z_ref[...] = jnp.zeros_like(z_ref)

    row2 = lax.broadcasted_iota(jnp.int32, (2 * C, W), 0)
    col2 = lax.broadcasted_iota(jnp.int32, (2 * C, W), 1)
    order = (row2 % C) - (col2 % C)
    need = jnp.where(row2 >= C, 0, 1)
    gmask = (order >= need, -order >= need)
    left2 = (col2 < N) == (row2 < C)
    row1 = lax.broadcasted_iota(jnp.int32, (C, W), 0)
    col1 = lax.broadcasted_iota(jnp.int32, (C, W), 1)
    eye = jnp.where(col1 % N == row1, 1.0, 0.0)
    t1 = lax.broadcasted_iota(jnp.int32, (C, C), 0)
    i1 = lax.broadcasted_iota(jnp.int32, (C, C), 1)
    tri = (jnp.where(i1 <= t1, 1.0, 0.0).astype(BF16), jnp.where(i1 >= t1, 1.0, 0.0).astype(BF16))
    in_refs = ((rf_ref, vf_ref, knf_ref, lwf_ref, kdf_ref, bbf_ref),
               (rb_ref, vb_ref, knb_ref, lwb_ref, kdb_ref, bbb_ref))
    out_refs = (of_ref, ob_ref)
    slabs = [(bi, d) for bi in range(rf_ref.shape[0]) for d in range(2)]

    def bd(x):
        xb = x.astype(BF16)
        return jnp.where(left2, jnp.concatenate([xb, xb], axis=0), jnp.zeros((), BF16))

    def chunk(cc, carry):
        rows_d = (pl.ds(pl.multiple_of(cc * C, C), C),
                  pl.ds(pl.multiple_of((n_chunks - 1 - cc) * C, C), C))
        wide = []
        for bi, d in slabs:
            r_ref, v_ref, kn_ref, lw_ref, kd_ref, bb_ref = in_refs[d]
            rows = rows_d[d]
            r = r_ref[bi, rows, :].astype(F32)
            v = v_ref[bi, rows, :]
            kn = kn_ref[bi, rows, :].astype(F32)
            lw = lw_ref[bi, 0, rows, :]
            kd = kd_ref[bi, 0, rows, :].astype(F32)
            b = bb_ref[bi, 0, rows, :].astype(F32)
            lc = _mm_exact_x(tri[d], lw)
            ltot = jnp.sum(lw, axis=0, keepdims=True)
            e_nlc = jnp.exp(-lc)
            e_t = jnp.exp(ltot - lc)
            wide.append(dict(
                at=(-kn * jnp.exp(lc - lw)).astype(BF16), rt=(r * jnp.exp(lc)).astype(BF16),
                bt=(b * e_nlc).astype(BF16), kt=(kd * e_nlc).astype(BF16),
                bh=(b * e_t).astype(BF16), kh=(kd * e_t).astype(BF16),
                v=v.astype(BF16), gam=jnp.exp(ltot)))
        chains = [(s, p) for s in range(len(slabs)) for p in range(P)]
        ps = lambda s, p, name: wide[s][name][:, p * W:(p + 1) * W]
        mm = lambda a, bmat: jnp.dot(a.astype(BF16), bmat, preferred_element_type=F32)
        dmask = lambda s: gmask[slabs[s][1]]

        x = [jnp.concatenate([ps(d, p, "at"), ps(d, p, "rt")], axis=0) for d, p in chains]
        gb = [jnp.where(dmask(d), _mm_nt(xi, bd(ps(d, p, "bt"))), 0.0) for xi, (d, p) in zip(x, chains)]
        gk = [jnp.where(dmask(d), _mm_nt(xi, bd(ps(d, p, "kt"))), 0.0) for xi, (d, p) in zip(x, chains)]
        l1 = [g[:C] for g in gb]
        tinv = [eye + l for l in l1]
        lp = [mm(l, bd(l)) for l in l1]
        n_lvl = int(math.log2(C))
        for lvl in range(2, n_lvl):
            both = [mm(jnp.concatenate([l, t], axis=0), bd(l)) for l, t in zip(lp, tinv)]
            tinv = [t + bo[C:] for t, bo in zip(tinv, both)]
            lp = [bo[:C] for bo in both]
        tinv = [t + mm(t, bd(l)) for t, l in zip(tinv, lp)]
        ah = [mm(t, bd(ps(d, p, "at"))) for t, (d, p) in zip(tinv, chains)]
        w = [mm(g[:C], bd(ps(d, p, "v"))) for g, (d, p) in zip(gk, chains)]
        vh = [mm(t, bd(wi)) for t, wi in zip(tinv, w)]
        z = [z_ref[n] for n in range(len(chains))]
        az = [_mm_nt(jnp.concatenate([a.astype(BF16), ps(d, p, "rt")], axis=0), zi)
              for a, zi, (d, p) in zip(ah, z, chains)]
        u = [(azi[:C] + vhi).astype(BF16) for azi, vhi in zip(az, vh)]
        o = [azi[C:] + mm(g1[C:], bd(ui)) + mm(g2[C:], bd(ps(d, p, "v")))
             for azi, g1, g2, ui, (d, p) in zip(az, gb, gk, u, chains)]
        znew = [zi * wide[d]["gam"][:, p * W:(p + 1) * W]
                + jnp.where(left2, _mm_tn(jnp.concatenate([ui, ps(d, p, "v")], axis=0),
                                          jnp.concatenate([ps(d, p, "bh"), ps(d, p, "kh")], axis=0)), 0.0)
                for zi, ui, (d, p) in zip(z, u, chains)]
        z_ref[...] = jnp.stack(znew, axis=0)
        for s, (bi, d) in enumerate(slabs):
            out_refs[d][bi, rows_d[d], :] = jnp.concatenate(o[s * P:(s + 1) * P], axis=1).astype(
                out_refs[d].dtype)
        return carry

    lax.fori_loop(0, n_chunks, chunk, 0)


def _rw_scan(r, v, kn, lw, kd, bb, *, tt, nb):
    B, S, C = r.shape
    nT = S // tt
    f1 = pl.BlockSpec((nb, tt, C), lambda b, j: (b, j, 0))
    b1 = pl.BlockSpec((nb, tt, C), lambda b, j: (b, nT - 1 - j, 0))
    f2 = pl.BlockSpec((nb, 1, tt, C), lambda b, j: (b, 0, j, 0))
    b2 = pl.BlockSpec((nb, 1, tt, C), lambda b, j: (b, 1, nT - 1 - j, 0))
    out = jax.ShapeDtypeStruct((B, S, C), BF16)
    return pl.pallas_call(
        functools.partial(_rw_scan_kernel, n_chunks=tt // RW_CHUNK),
        grid=(B // nb, nT),
        in_specs=[f1, f1, f1, f2, f2, f2, b1, b1, b1, b2, b2, b2],
        out_specs=[f1, b1],
        out_shape=[out, out],
        scratch_shapes=[pltpu.VMEM((nb * RW_HEADS, 2 * RW_HEAD_DIM, 2 * RW_HEAD_DIM), F32)],
        compiler_params=_cparams(("parallel", "arbitrary")),
        name="rw_scan",
    )(r, v, kn, lw, kd, bb, r, v, kn, lw, kd, bb)


CV_HALO = 16


def _cv_kernel(p_ref, pp_ref, pn_ref, dw_ref, db_ref, lg_ref, lb_ref, o_ref, u_ref):
    i = pl.program_id(1)
    n = pl.num_programs(1)
    ts = p_ref.shape[1]
    Cc = CV_CHANNELS

    def glu(x):
        x = x.astype(F32)
        return x[:, :Cc] * _sigmoid(x[:, Cc:])

    u_ref[CV_HALO:CV_HALO + ts, :] = glu(p_ref[0])
    u_ref[0:CV_HALO, :] = jnp.where(i > 0, glu(pp_ref[0]), 0.0)
    u_ref[CV_HALO + ts:, :] = jnp.where(i < n - 1, glu(pn_ref[0]), 0.0)
    base = CV_HALO - CONV_TAPS // 2
    ext = ts + 8
    acc = None
    for res in range(8):
        part = None
        for s in range(res, base + CONV_TAPS, 8):
            if s < base:
                continue
            term = u_ref[s - res:s - res + ext, :] * dw_ref[s - base:s - base + 1, :]
            part = term if part is None else part + term
        if part is None:
            continue
        shifted = part[:ts] if res == 0 else pltpu.roll(part, ext - res, 0)[:ts]
        acc = shifted if acc is None else acc + shifted
    acc = acc + db_ref[...]
    mu = jnp.mean(acc, axis=-1, keepdims=True)
    xc = acc - mu
    var = jnp.mean(xc * xc, axis=-1, keepdims=True)
    y = xc * lax.rsqrt(var + CV_LN_EPS) * lg_ref[...] + lb_ref[...]
    o_ref[0] = (y * _sigmoid(y)).astype(o_ref.dtype)


def _cv(p_cv, dw, db, lg, lb, *, ts):
    B, S, _ = p_cv.shape
    nh = S // CV_HALO
    return pl.pallas_call(
        _cv_kernel,
        grid=(B, S // ts),
        in_specs=[
            pl.BlockSpec((1, ts, CV_COLS), lambda b, i: (b, i, 0)),
            pl.BlockSpec((1, CV_HALO, CV_COLS),
                         lambda b, i: (b, jnp.maximum(i * (ts // CV_HALO) - 1, 0), 0)),
            pl.BlockSpec((1, CV_HALO, CV_COLS),
                         lambda b, i: (b, jnp.minimum((i + 1) * (ts // CV_HALO), nh - 1), 0)),
            _const_spec(dw.shape), _const_spec(db.shape), _const_spec(lg.shape), _const_spec(lb.shape),
        ],
        out_specs=pl.BlockSpec((1, ts, CV_CHANNELS), lambda b, i: (b, i, 0)),
        out_shape=jax.ShapeDtypeStruct((B, S, CV_CHANNELS), BF16),
        scratch_shapes=[pltpu.VMEM((ts + 2 * CV_HALO, CV_CHANNELS), F32)],
        compiler_params=_cparams(("parallel", "parallel")),
        name="cv_branch",
    )(p_cv, p_cv, p_cv, dw, db, lg, lb)


def _mla_prep_kernel(p_ref, cb_ref, s1_ref, s2_ref, qn_ref, wq_ref, wqs_ref, kvn_ref, wkv_ref,
                     gq_ref, gqs_ref, gk_ref, q_o, k_o, v_o):
    p = p_ref[0].astype(F32)
    cb = cb_ref[0]
    s1 = s1_ref[0]
    s2 = s2_ref[0]
    s12 = s1 + s2
    cq = p[:, 0:MLA_Q_RANK]
    ckv = p[:, MLA_Q_RANK:MLA_Q_RANK + MLA_KV_RANK]
    kr = p[:, MLA_Q_RANK + MLA_KV_RANK:MLA_COLS_PAD]
    cqn = _rms(cq, qn_ref[...]).astype(BF16)
    qall = jnp.dot(cqn, wq_ref[...], preferred_element_type=F32)
    qswp = jnp.dot(cqn, wqs_ref[...], preferred_element_type=F32)
    kvall = _mm(_rms(ckv, kvn_ref[...]), wkv_ref[...])
    gq = gq_ref[...]
    gqs = gqs_ref[...]
    gk = gk_ref[...]
    lane = lax.broadcasted_iota(jnp.int32, (1, LANE), 1)
    ones_col = jnp.where(lane == MLA_V, 1.0, 0.0)
    scale = MLA_QK ** -0.5
    HB = MLA_HEADS * LANE

    krot = pltpu.roll(kr, MLA_NOPE, 1)
    kr_ss = jnp.sum(krot * krot, axis=-1, keepdims=True)
    krg = krot * gk
    kr_r = (krg * cb + pltpu.roll(krg, LANE - MLA_ROPE // 2, 1) * s1
            + pltpu.roll(krg, MLA_ROPE // 2, 1) * s2)

    for h in range(MLA_HEADS):
        blk = slice(h * LANE, (h + 1) * LANE)
        qb = qall[:, blk]
        rq = lax.rsqrt(jnp.sum(qb * qb, axis=-1, keepdims=True) * (1.0 / MLA_QK) + NORM_EPS)
        q_o[0, h] = (((qb * gq) * cb + (qswp[:, blk] * gqs) * s12) * (rq * scale)).astype(q_o.dtype)
        kb = kvall[:, blk]
        rk = lax.rsqrt((jnp.sum(kb * kb, axis=-1, keepdims=True) + kr_ss) * (1.0 / MLA_QK) + NORM_EPS)
        k_o[0, h] = ((kb * gk + kr_r) * rk).astype(k_o.dtype)
        v_o[0, h] = (kvall[:, HB + h * LANE:HB + (h + 1) * LANE] + ones_col).astype(v_o.dtype)


def _mla_prep(p_mla, tabs, wts, *, ts):
    B, S, _ = p_mla.shape
    H = MLA_HEADS
    tile = lambda b, i: (b, i, 0)
    tspec = pl.BlockSpec((1, ts, LANE), tile)
    names = ["q_norm", "wq", "wq_swap", "kv_norm", "wkv", "gq", "gq_swap", "gk"]
    hspec = pl.BlockSpec((1, H, ts, LANE), lambda b, i: (b, 0, i, 0))
    hshape = jax.ShapeDtypeStruct((B, H, S, LANE), BF16)
    return pl.pallas_call(
        _mla_prep_kernel,
        grid=(B, S // ts),
        in_specs=[pl.BlockSpec((1, ts, MLA_COLS_PAD), tile), tspec, tspec, tspec]
        + [_const_spec(wts[n].shape) for n in names],
        out_specs=[hspec, hspec, hspec],
        out_shape=[hshape, hshape, hshape],
        compiler_params=_cparams(("parallel", "parallel")),
        name="mla_prep",
    )(p_mla, *tabs, *[wts[n] for n in names])


ATTN_SUB = 256


def _attn_kernel(q_ref, k_ref, v_ref, o_ref):
    tq = q_ref.shape[2]
    for r0 in range(0, tq, ATTN_SUB):
        outs = []
        for h in range(2):
            s = _mm_nt(q_ref[0, h, r0:r0 + ATTN_SUB, :], k_ref[0, h])
            m = jnp.max(s, axis=-1, keepdims=True)
            p = jnp.exp(s - m).astype(BF16)
            ov = jnp.dot(p, v_ref[0, h], preferred_element_type=F32)
            outs.append(ov[:, :MLA_V] / ov[:, MLA_V:MLA_V + 1])
        o_ref[0, r0:r0 + ATTN_SUB, :] = jnp.concatenate(outs, axis=1).astype(o_ref.dtype)


def _attn(q, k, v, *, tq):
    B, H, S, _ = q.shape
    return pl.pallas_call(
        _attn_kernel,
        grid=(B, H // 2, S // tq),
        in_specs=[
            pl.BlockSpec((1, 2, tq, LANE), lambda b, h, i: (b, h, i, 0)),
            pl.BlockSpec((1, 2, S, LANE), lambda b, h, i: (b, h, 0, 0)),
            pl.BlockSpec((1, 2, S, LANE), lambda b, h, i: (b, h, 0, 0)),
        ],
        out_specs=pl.BlockSpec((1, tq, 2 * MLA_V), lambda b, h, i: (b, i, h)),
        out_shape=jax.ShapeDtypeStruct((B, S, H * MLA_V), BF16),
        compiler_params=_cparams(("parallel", "parallel", "parallel")),
        name="mla_attn",
    )(q, k, v)


def _merge_kernel(x_ref, of_ref, ob_ref, g_ref, bn_ref, ucv_ref, omla_ref, gate_ref,
                  avg_ref, gng_ref, gnb_ref, wrw_ref, wcv_ref, wmla_ref, wo_ref, out_ref):
    D = D_MODEL
    o = of_ref[...].astype(F32) + ob_ref[...].astype(F32)
    avg = avg_ref[...]
    inv_n = 1.0 / RW_HEAD_DIM
    mu = _mm_x_exact(o, avg) * inv_n
    xc = o - mu
    var = _mm_x_exact(xc * xc, avg) * inv_n
    on = xc * lax.rsqrt(var + RW_GN_EPS) * gng_ref[...] + gnb_ref[...]
    orw = (on + bn_ref[...].astype(F32)) * g_ref[...].astype(F32)
    gate = gate_ref[...].astype(F32)
    merged = (_sigmoid(gate[:, 0:D]) * _mm(orw, wrw_ref[...])
              + _sigmoid(gate[:, D:2 * D]) * jnp.dot(ucv_ref[...], wcv_ref[...], preferred_element_type=F32)
              + _sigmoid(gate[:, 2 * D:3 * D]) * jnp.dot(omla_ref[...], wmla_ref[...],
                                                         preferred_element_type=F32))
    out_ref[...] = x_ref[...] + _mm(merged, wo_ref[...])


def _merge(x2, o_f, o_b, g, bn, ucv, omla, gate, wts, *, tm):
    T, D = x2.shape
    C = RW_WIDTH
    row = lambda n: pl.BlockSpec((tm, n), lambda i: (i, 0))
    names = ["e64", "gn_g", "gn_b", "w_rw", "w_cv", "w_mla", "w_o"]
    return pl.pallas_call(
        _merge_kernel,
        grid=(T // tm,),
        in_specs=[row(D), row(C), row(C), row(C), row(C), row(C), row(C),
                  row(GATE_COLS)] + [_const_spec(wts[n].shape) for n in names],
        out_specs=row(D),
        out_shape=jax.ShapeDtypeStruct((T, D), F32),
        compiler_params=_cparams(("parallel",)),
        name="merge",
    )(x2, o_f, o_b, g, bn, ucv, omla, gate, *[wts[n] for n in names])


def _block_diag2(w):
    z = jnp.zeros_like(w[0])
    return jnp.concatenate([jnp.concatenate([w[0], z], axis=1), jnp.concatenate([z, w[1]], axis=1)], axis=0)


def _head_block_cols(w, width):
    R = w.shape[0]
    w = w.reshape(R, MLA_HEADS, width)
    return jnp.pad(w, ((0, 0), (0, 0), (0, LANE - width))).reshape(R, MLA_HEADS * LANE)


def _rope_tables(positions):
    half = MLA_ROPE // 2
    inv = ROPE_THETA ** (-jnp.arange(0, MLA_ROPE, 2, dtype=F32) / MLA_ROPE)
    ang = positions.astype(F32)[..., None] * inv
    cos, sin = jnp.cos(ang), jnp.sin(ang)
    B, S = positions.shape
    one = jnp.ones((B, S, MLA_NOPE), F32)
    z64 = jnp.zeros((B, S, MLA_NOPE), F32)
    z16 = jnp.zeros((B, S, half), F32)
    pad = jnp.zeros((B, S, LANE - MLA_QK), F32)
    cb = jnp.concatenate([one, cos, cos, pad + 1.0], axis=-1)
    s1 = jnp.concatenate([z64, -sin, z16, pad], axis=-1)
    s2 = jnp.concatenate([z64, z16, sin, pad], axis=-1)
    return cb, s1, s2


def _head_ones():
    hid = jnp.arange(RW_WIDTH) // RW_HEAD_DIM
    return (hid[:, None] == hid[None, :]).astype(BF16)


def _tile_sizes(T, S):
    pick = lambda n, cands: next(c for c in cands if n % c == 0)
    return dict(
        ffn_tm=pick(T, (2048, 1024, 512, 256, 128, 64, 32, 16, 8)),
        ffn_tf=pick(D_FF, (256, 128)),
        tok_tm=pick(T, (512, 256, 128, 64, 32, 16, 8)),
        seq_ts=pick(S, (512, 256, 128, 64)),
        scan_tt=pick(S, (512, 256, 128, 64)),
        attn_tq=pick(S, (1024, 512, 256)),
    )


def kernel(x, positions, norm_ffn1, ffn1_w1, ffn1_w3, ffn1_w2, norm_mix, w_in, rw_mu, rw_w0, rw_w_up, rw_a0, rw_a_up, rw_g_up, rw_k_k, rw_k_a, rw_r_k, rw_v0, rw_v_down, rw_v_up, rw_gn_g, rw_gn_b, rw_w_branch, cv_dw_w, cv_dw_b, cv_ln_g, cv_ln_b, cv_w_branch, mla_q_norm, mla_w_uq, mla_kv_norm, mla_w_ukv, mla_qk_q_g, mla_qk_k_g, mla_w_branch, w_o, norm_ffn2, ffn2_w1, ffn2_w3, ffn2_w2):
    B, S, D = x.shape
    T = B * S
    L = w_in.shape[0]
    ts_ = _tile_sizes(T, S)
    tabs = _rope_tables(positions)
    e64 = _head_ones()
    row = lambda a: a.reshape(1, -1)
    x2 = x.reshape(T, D)
    v_first = None
    for i in range(L):
        x2 = _ffn(x2, row(norm_ffn1[i]), ffn1_w1[i].astype(BF16), ffn1_w3[i].astype(BF16),
                  ffn1_w2[i].astype(BF16), tm=ts_["ffn_tm"], tf=ts_["ffn_tf"])

        c1, c2, c3 = RW_COLS, RW_COLS + CV_COLS, RW_COLS + CV_COLS + MLA_COLS
        wi = w_in[i]
        vdw = rw_v_down[i - 1] if i > 0 else jnp.zeros((D, RW_VRES_RANK), F32)
        w_cat = jnp.concatenate([
            wi[:, :c1], wi[:, c1:c2],
            jnp.pad(wi[:, c2:c3], ((0, 0), (0, MLA_COLS_PAD - MLA_COLS))),
            wi[:, c3:], jnp.pad(vdw, ((0, 0), (0, VD_COLS_PAD - RW_VRES_RANK)))], axis=1).astype(BF16)
        p_rw, p_cv, p_mla, p_gate, p_vd = _inproj(x2, row(norm_mix[i]), w_cat, tm=ts_["tok_tm"])

        rw_wts = dict(
            mu=row(rw_mu[i]), wup=_block_diag2(rw_w_up[i]).astype(BF16), w0=row(rw_w0[i]),
            aup=_block_diag2(rw_a_up[i]).astype(BF16), a0=row(rw_a0[i]), gup=rw_g_up[i].astype(BF16),
            k_k=row(rw_k_k[i]), k_a=row(rw_k_a[i]), r_k=row(rw_r_k[i]), e64=e64)
        if i > 0:
            rw_wts["v0"] = row(rw_v0[i - 1])
            rw_wts["vup"] = jnp.pad(rw_v_up[i - 1], ((0, VD_COLS_PAD - RW_VRES_RANK), (0, 0))).astype(BF16)
        r, v, kn, lw, kd, bb, g, bn = _rw_prep(
            p_rw.reshape(B, S, RW_COLS), p_vd if i > 0 else None, v_first, rw_wts, ts=ts_["seq_ts"])
        if i == 0:
            v_first = v
        o_f, o_b = _rw_scan(r, v, kn, lw, kd, bb, tt=ts_["scan_tt"],
                            nb=2 if B % 2 == 0 else 1)

        ucv = _cv(p_cv.reshape(B, S, CV_COLS), cv_dw_w[i], row(cv_dw_b[i]), row(cv_ln_g[i]),
                  row(cv_ln_b[i]), ts=ts_["seq_ts"])

        wq = mla_w_uq[i].reshape(MLA_Q_RANK, MLA_HEADS, MLA_QK)
        wkv = mla_w_ukv[i].reshape(MLA_KV_RANK, MLA_HEADS, MLA_NOPE + MLA_V)
        gpad = lambda gvec: jnp.pad(gvec, (0, LANE - MLA_QK)).reshape(1, LANE)
        half = MLA_ROPE // 2
        swap = lambda a: jnp.concatenate(
            [a[..., :MLA_NOPE], a[..., MLA_NOPE + half:], a[..., MLA_NOPE:MLA_NOPE + half]], axis=-1)
        mla_wts = dict(
            q_norm=row(mla_q_norm[i]),
            wq=_head_block_cols(wq.reshape(MLA_Q_RANK, -1), MLA_QK).astype(BF16),
            wq_swap=_head_block_cols(swap(wq).reshape(MLA_Q_RANK, -1), MLA_QK).astype(BF16),
            gq_swap=gpad(swap(mla_qk_q_g[i])),
            kv_norm=row(mla_kv_norm[i]),
            wkv=jnp.concatenate([
                _head_block_cols(wkv[:, :, :MLA_NOPE].reshape(MLA_KV_RANK, -1), MLA_NOPE),
                _head_block_cols(wkv[:, :, MLA_NOPE:].reshape(MLA_KV_RANK, -1), MLA_V)], axis=1).astype(BF16),
            gq=gpad(mla_qk_q_g[i]), gk=gpad(mla_qk_k_g[i]))
        qh, kh, vh = _mla_prep(p_mla.reshape(B, S, MLA_COLS_PAD), tabs, mla_wts, ts=ts_["seq_ts"])
        omla = _attn(qh, kh, vh, tq=ts_["attn_tq"])

        m_wts = dict(e64=e64, gn_g=row(rw_gn_g[i]), gn_b=row(rw_gn_b[i]),
                     w_rw=rw_w_branch[i].astype(BF16), w_cv=cv_w_branch[i].astype(BF16),
                     w_mla=mla_w_branch[i].astype(BF16), w_o=w_o[i].astype(BF16))
        x2 = _merge(x2, o_f.reshape(T, -1), o_b.reshape(T, -1), g.reshape(T, -1), bn.reshape(T, -1), ucv.reshape(T, -1),
                    omla.reshape(T, -1), p_gate, m_wts, tm=ts_["tok_tm"])

        x2 = _ffn(x2, row(norm_ffn2[i]), ffn2_w1[i].astype(BF16), ffn2_w3[i].astype(BF16),
                  ffn2_w2[i].astype(BF16), tm=ts_["ffn_tm"], tf=ts_["ffn_tf"])
    return x2.reshape(B, S, D)
```

```python
import functools
import math

import jax
import jax.numpy as jnp
from jax import lax
from jax.experimental import pallas as pl
from jax.experimental.pallas import tpu as pltpu

F32 = jnp.float32
BF16 = jnp.bfloat16

D_MODEL = 1024
D_FF = 2816
NORM_EPS = 1e-6

RW_HEADS = 8
RW_HEAD_DIM = 64
RW_WIDTH = RW_HEADS * RW_HEAD_DIM
RW_DECAY_RANK = 64
RW_ICL_RANK = 64
RW_VRES_RANK = 32
RW_GATE_RANK = 128
RW_GN_EPS = 64e-5
RW_COLS = 3 * RW_WIDTH + 2 * RW_DECAY_RANK + 2 * RW_ICL_RANK + RW_GATE_RANK

CV_CHANNELS = 512
CONV_TAPS = 31
CV_LN_EPS = 1e-5
CV_COLS = 2 * CV_CHANNELS

MLA_HEADS = 8
MLA_Q_RANK = 384
MLA_KV_RANK = 256
MLA_NOPE = 64
MLA_ROPE = 32
MLA_QK = MLA_NOPE + MLA_ROPE
MLA_V = 64
ROPE_THETA = 10000.0
MLA_COLS = MLA_Q_RANK + MLA_KV_RANK + MLA_ROPE
MLA_COLS_PAD = 768
GATE_COLS = 3 * D_MODEL
VD_COLS_PAD = 128

LANE = 128
VMEM_LIMIT = 56 * 1024 * 1024

RW_CHUNK = 64


def _cparams(sem):
    return pltpu.CompilerParams(dimension_semantics=sem, vmem_limit_bytes=VMEM_LIMIT)


def _wspec(w, **kw):
    arr, layer = w
    if layer is None:
        n = arr.ndim
        return pl.BlockSpec(arr.shape, lambda *_: (0,) * n, **kw)
    n = arr.ndim - 1
    return pl.BlockSpec((None,) + arr.shape[1:], lambda *_: (layer,) + (0,) * n, **kw)


def _mm(a, b):
    return jnp.dot(a.astype(BF16), b.astype(BF16), preferred_element_type=F32)


def _mm_nt(a, b):
    return lax.dot_general(a.astype(BF16), b.astype(BF16), (((1,), (1,)), ((), ())),
                           preferred_element_type=F32)


def _mm_tn(a, b):
    return lax.dot_general(a.astype(BF16), b.astype(BF16), (((0,), (0,)), ((), ())),
                           preferred_element_type=F32)


def _split3(x):
    hi = x.astype(BF16)
    r1 = x - hi.astype(F32)
    mid = r1.astype(BF16)
    lo = (r1 - mid.astype(F32)).astype(BF16)
    return hi, mid, lo


def _mm_x_exact(x, m):
    hi = x.astype(BF16)
    lo = (x - hi.astype(F32)).astype(BF16)
    return jnp.dot(hi, m, preferred_element_type=F32) + jnp.dot(lo, m, preferred_element_type=F32)


def _mm_exact_x(m, x):
    hi, mid, lo = _split3(x)
    return (jnp.dot(m, hi, preferred_element_type=F32) + jnp.dot(m, mid, preferred_element_type=F32)
            + jnp.dot(m, lo, preferred_element_type=F32))


def _sigmoid(x):
    return 1.0 / (1.0 + jnp.exp(-x))


def _rms(x, g, eps=NORM_EPS):
    return x * lax.rsqrt(jnp.mean(x * x, axis=-1, keepdims=True) + eps) * g


def _ffn_kernel(x_ref, g_ref, w1_ref, w3_ref, w2_ref, o_ref, h_ref, *, tf):
    x = x_ref[...]
    h_ref[...] = _rms(x, g_ref[...]).astype(BF16)
    o_ref[...] = x
    n_slab = w1_ref.shape[1] // tf

    def slab(j, carry):
        c = pl.ds(pl.multiple_of(j * tf, tf), tf)
        h = h_ref[...]
        a = jnp.dot(h, w1_ref[:, c], preferred_element_type=F32)
        b = jnp.dot(h, w3_ref[:, c], preferred_element_type=F32)
        u = (a * _sigmoid(a) * b).astype(BF16)
        o_ref[...] += 0.5 * jnp.dot(u, w2_ref[c, :], preferred_element_type=F32)
        return carry

    lax.fori_loop(0, n_slab, slab, 0)


def _ffn(x2, g, w1, w3, w2, *, tm, tf):
    T, D = x2.shape
    once = dict(pipeline_mode=pl.Buffered(1))
    return pl.pallas_call(
        functools.partial(_ffn_kernel, tf=tf),
        grid=(T // tm,),
        in_specs=[pl.BlockSpec((tm, D), lambda i: (i, 0)), _wspec(g),
                  _wspec(w1, **once), _wspec(w3, **once), _wspec(w2, **once)],
        out_specs=pl.BlockSpec((tm, D), lambda i: (i, 0)),
        out_shape=jax.ShapeDtypeStruct((T, D), F32),
        scratch_shapes=[pltpu.VMEM((tm, D), BF16)],
        compiler_params=_cparams(("parallel",)),
        name="ffn",
    )(x2, g[0], w1[0], w3[0], w2[0])


def _inproj_kernel(x_ref, g_ref, w_ref, rw_ref, cv_ref, mla_ref, gate_ref, vd_ref):
    h = _rms(x_ref[...], g_ref[...]).astype(BF16)
    c = 0
    for ref in (rw_ref, cv_ref, mla_ref, gate_ref, vd_ref):
        n = ref.shape[1]
        ref[...] = jnp.dot(h, w_ref[:, c:c + n], preferred_element_type=F32).astype(ref.dtype)
        c += n


def _inproj(x2, g, w_cat, *, tm):
    T, D = x2.shape
    widths = (RW_COLS, CV_COLS, MLA_COLS_PAD, GATE_COLS, VD_COLS_PAD)
    assert w_cat[0].shape[-1] == sum(widths)
    return pl.pallas_call(
        _inproj_kernel,
        grid=(T // tm,),
        in_specs=[pl.BlockSpec((tm, D), lambda i: (i, 0)), _wspec(g), _wspec(w_cat)],
        out_specs=[pl.BlockSpec((tm, n), lambda i: (i, 0)) for n in widths],
        out_shape=[jax.ShapeDtypeStruct((T, n), dt)
                   for n, dt in zip(widths, (F32, BF16, BF16, BF16, F32))],
        compiler_params=_cparams(("parallel",)),
        name="inproj",
    )(x2, g[0], w_cat[0])


def _rw_prep_kernel(*refs, has_vres):
    if has_vres:
        (p_ref, pp_ref, pn_ref, vd_ref, vf_ref, mu_ref, wup_ref, w0_ref, aup_ref, a0_ref, gup_ref,
         kk_ref, ka_ref, rk_ref, e_ref, v0_ref, vup_ref,
         r_o, v_o, kn_o, lw_o, kd_o, ic_o, g_o, bn_o) = refs
    else:
        (p_ref, pp_ref, pn_ref, mu_ref, wup_ref, w0_ref, aup_ref, a0_ref, gup_ref,
         kk_ref, ka_ref, rk_ref, e_ref,
         r_o, v_o, kn_o, lw_o, kd_o, ic_o, g_o, bn_o) = refs
    i = pl.program_id(1)
    n = pl.num_programs(1)
    C = RW_WIDTH
    p = p_ref[0]
    ts = p.shape[0]
    prev_row = jnp.where(i > 0, pp_ref[0, 7:8, :], 0.0)
    next_row = jnp.where(i < n - 1, pn_ref[0, 0:1, :], 0.0)
    rows = lax.broadcasted_iota(jnp.int32, (ts, 1), 0)
    prev = jnp.where(rows == 0, prev_row, pltpu.roll(p, 1, 0))
    nxt = jnp.where(rows == ts - 1, next_row, pltpu.roll(p, ts - 1, 0))
    pm = p + (0.5 * (prev + nxt) - p) * mu_ref[...]

    r = pm[:, 0:C]
    k = pm[:, C:2 * C]
    v = pm[:, 2 * C:3 * C]
    o1 = 3 * C
    wd = jnp.tanh(pm[:, o1:o1 + 2 * RW_DECAY_RANK])
    o2 = o1 + 2 * RW_DECAY_RANK
    ad = pm[:, o2:o2 + 2 * RW_ICL_RANK]
    o3 = o2 + 2 * RW_ICL_RANK
    gd = pm[:, o3:o3 + RW_GATE_RANK]

    wl = w0_ref[...] + _mm(wd, wup_ref[...])
    lw = (-math.exp(-0.5)) * _sigmoid(wl)
    icl = _sigmoid(a0_ref[...] + _mm(ad, aup_ref[...]))
    g = _mm(_sigmoid(gd), gup_ref[...])

    if has_vres:
        nu = _sigmoid(v0_ref[...] + _mm(vd_ref[...], vup_ref[...]))
        v = v + (vf_ref[0] - v) * nu

    e = e_ref[...]
    kkr = k * kk_ref[...]
    ss = _mm_x_exact(kkr * kkr, e)
    kn = kkr * lax.rsqrt(jnp.maximum(ss, 1e-24))

    ka = ka_ref[...]
    kd0 = k * (1.0 + (icl[:, 0:C] - 1.0) * ka)
    kd1 = k * (1.0 + (icl[:, C:2 * C] - 1.0) * ka)
    kb = 0.5 * (kd0 + kd1)
    sb = _mm_x_exact(r * kb * rk_ref[...], e)

    r_o[0] = r.astype(r_o.dtype)
    v_o[0] = v.astype(v_o.dtype)
    kn_o[0] = kn.astype(kn_o.dtype)
    lw_o[0, 0] = lw[:, 0:C]
    lw_o[0, 1] = lw[:, C:2 * C]
    kd_o[0, 0] = kd0.astype(kd_o.dtype)
    kd_o[0, 1] = kd1.astype(kd_o.dtype)
    ic_o[0, 0] = (kn * icl[:, 0:C]).astype(ic_o.dtype)
    ic_o[0, 1] = (kn * icl[:, C:2 * C]).astype(ic_o.dtype)
    g_o[0] = g.astype(g_o.dtype)
    bn_o[0] = (sb * v).astype(bn_o.dtype)


def _rw_prep(p_rw, vd, v_first, wts, *, ts):
    B, S, _ = p_rw.shape
    C = RW_WIDTH
    has_vres = vd is not None
    nblk8 = S // 8
    tile = lambda b, i: (b, i, 0)
    in_arrays = [p_rw, p_rw, p_rw]
    in_specs = [
        pl.BlockSpec((1, ts, RW_COLS), tile),
        pl.BlockSpec((1, 8, RW_COLS), lambda b, i: (b, jnp.maximum(i * (ts // 8) - 1, 0), 0)),
        pl.BlockSpec((1, 8, RW_COLS), lambda b, i: (b, jnp.minimum((i + 1) * (ts // 8), nblk8 - 1), 0)),
    ]
    if has_vres:
        in_arrays += [vd, v_first]
        in_specs += [pl.BlockSpec((ts, VD_COLS_PAD), lambda b, i: (b * (S // ts) + i, 0)),
                     pl.BlockSpec((1, ts, C), tile)]
    names = ["mu", "wup", "w0", "aup", "a0", "gup", "k_k", "k_a", "r_k", "e64"]
    if has_vres:
        names += ["v0", "vup"]
    for nme in names:
        in_arrays.append(wts[nme][0])
        in_specs.append(_wspec(wts[nme]))
    one = jax.ShapeDtypeStruct((B, S, C), BF16)
    two = jax.ShapeDtypeStruct((B, 2, S, C), BF16)
    two_f32 = jax.ShapeDtypeStruct((B, 2, S, C), F32)
    ospec1 = pl.BlockSpec((1, ts, C), tile)
    ospec2 = pl.BlockSpec((1, 2, ts, C), lambda b, i: (b, 0, i, 0))
    return pl.pallas_call(
        functools.partial(_rw_prep_kernel, has_vres=has_vres),
        grid=(B, S // ts),
        in_specs=in_specs,
        out_specs=[ospec1, ospec1, ospec1, ospec2, ospec2, ospec2, ospec1, ospec1],
        out_shape=[one, one, one, two_f32, two, two, one, one],
        compiler_params=_cparams(("parallel", "parallel")),
        name="rw_prep",
    )(*in_arrays)


def _rw_scan_kernel(rf_ref, vf_ref, knf_ref, lwf_ref, kdf_ref, bbf_ref,
                    rb_ref, vb_ref, knb_ref, lwb_ref, kdb_ref, bbb_ref,
                    of_ref, ob_ref, z_ref, *, n_chunks):
    j = pl.program_id(1)
    C = RW_CHUNK
    N = RW_HEAD_DIM
    P = RW_HEADS // 2
    W = 2 * N

    @pl.when(j == 0)
    def _():
        z_ref[...] = jnp.zeros_like(z_ref)

    row2 = lax.broadcasted_iota(jnp.int32, (2 * C, W), 0)
    col2 = lax.broadcasted_iota(jnp.int32, (2 * C, W), 1)
    order = (row2 % C) - (col2 % C)
    need = jnp.where(row2 >= C, 0, 1)
    gmask = (order >= need, -order >= need)
    left2 = (col2 < N) == (row2 < C)
    row1 = lax.broadcasted_iota(jnp.int32, (C, W), 0)
    col1 = lax.broadcasted_iota(jnp.int32, (C, W), 1)
    eye = jnp.where(col1 % N == row1, 1.0, 0.0)
    t1 = lax.broadcasted_iota(jnp.int32, (C, C), 0)
    i1 = lax.broadcasted_iota(jnp.int32, (C, C), 1)
    tri = (jnp.where(i1 <= t1, 1.0, 0.0).astype(BF16), jnp.where(i1 >= t1, 1.0, 0.0).astype(BF16))
    in_refs = ((rf_ref, vf_ref, knf_ref, lwf_ref, kdf_ref, bbf_ref),
               (rb_ref, vb_ref, knb_ref, lwb_ref, kdb_ref, bbb_ref))
    out_refs = (of_ref, ob_ref)
    slabs = [(bi, d) for bi in range(rf_ref.shape[0]) for d in range(2)]

    def bd(x):
        xb = x.astype(BF16)
        return jnp.where(left2, jnp.concatenate([xb, xb], axis=0), jnp.zeros((), BF16))

    def chunk(cc, carry):
        rows_d = (pl.ds(pl.multiple_of(cc * C, C), C),
                  pl.ds(pl.multiple_of((n_chunks - 1 - cc) * C, C), C))
        wide = []
        for bi, d in slabs:
            r_ref, v_ref, kn_ref, lw_ref, kd_ref, bb_ref = in_refs[d]
            rows = rows_d[d]
            r = r_ref[bi, rows, :].astype(F32)
            v = v_ref[bi, rows, :]
            kn = kn_ref[bi, rows, :].astype(F32)
            lw = lw_ref[bi, 0, rows, :]
            kd = kd_ref[bi, 0, rows, :].astype(F32)
            b = bb_ref[bi, 0, rows, :].astype(F32)
            lc = _mm_exact_x(tri[d], lw)
            ltot = jnp.sum(lw, axis=0, keepdims=True)
            e_nlc = jnp.exp(-lc)
            e_t = jnp.exp(ltot - lc)
            wide.append(dict(
                at=(-kn * jnp.exp(lc - lw)).astype(BF16), rt=(r * jnp.exp(lc)).astype(BF16),
                bt=(b * e_nlc).astype(BF16), kt=(kd * e_nlc).astype(BF16),
                bh=(b * e_t).astype(BF16), kh=(kd * e_t).astype(BF16),
                v=v.astype(BF16), gam=jnp.exp(ltot)))
        chains = [(s, p) for s in range(len(slabs)) for p in range(P)]
        ps = lambda s, p, name: wide[s][name][:, p * W:(p + 1) * W]
        mm = lambda a, bmat: jnp.dot(a.astype(BF16), bmat, preferred_element_type=F32)
        dmask = lambda s: gmask[slabs[s][1]]

        x = [jnp.concatenate([ps(d, p, "at"), ps(d, p, "rt")], axis=0) for d, p in chains]
        gb = [jnp.where(dmask(d), _mm_nt(xi, bd(ps(d, p, "bt"))), 0.0) for xi, (d, p) in zip(x, chains)]
        gk = [jnp.where(dmask(d), _mm_nt(xi, bd(ps(d, p, "kt"))), 0.0) for xi, (d, p) in zip(x, chains)]
        l1 = [g[:C] for g in gb]
        tinv = [eye + l for l in l1]
        lp = [mm(l, bd(l)) for l in l1]
        n_lvl = int(math.log2(C))
        for lvl in range(2, n_lvl):
            both = [mm(jnp.concatenate([l, t], axis=0), bd(l)) for l, t in zip(lp, tinv)]
            tinv = [t + bo[C:] for t, bo in zip(tinv, both)]
            lp = [bo[:C] for bo in both]
        tinv = [t + mm(t, bd(l)) for t, l in zip(tinv, lp)]
        ah = [mm(t, bd(ps(d, p, "at"))) for t, (d, p) in zip(tinv, chains)]
        w = [mm(g[:C], bd(ps(d, p, "v"))) for g, (d, p) in zip(gk, chains)]
        vh = [mm(t, bd(wi)) for t, wi in zip(tinv, w)]
        z = [z_ref[n] for n in range(len(chains))]
        az = [_mm_nt(jnp.concatenate([a.astype(BF16), ps(d, p, "rt")], axis=0), zi)
              for a, zi, (d, p) in zip(ah, z, chains)]
        u = [(azi[:C] + vhi).astype(BF16) for azi, vhi in zip(az, vh)]
        o = [azi[C:] + mm(g1[C:], bd(ui)) + mm(g2[C:], bd(ps(d, p, "v")))
             for azi, g1, g2, ui, (d, p) in zip(az, gb, gk, u, chains)]
        znew = [zi * wide[d]["gam"][:, p * W:(p + 1) * W]
                + jnp.where(left2, _mm_tn(jnp.concatenate([ui, ps(d, p, "v")], axis=0),
                                          jnp.concatenate([ps(d, p, "bh"), ps(d, p, "kh")], axis=0)), 0.0)
                for zi, ui, (d, p) in zip(z, u, chains)]
        z_ref[...] = jnp.stack(znew, axis=0)
        for s, (bi, d) in enumerate(slabs):
            out_refs[d][bi, rows_d[d], :] = jnp.concatenate(o[s * P:(s + 1) * P], axis=1).astype(
                out_refs[d].dtype)
        return carry

    lax.fori_loop(0, n_chunks, chunk, 0)


def _rw_scan(r, v, kn, lw, kd, bb, *, tt, nb):
    B, S, C = r.shape
    nT = S // tt
    f1 = pl.BlockSpec((nb, tt, C), lambda b, j: (b, j, 0))
    b1 = pl.BlockSpec((nb, tt, C), lambda b, j: (b, nT - 1 - j, 0))
    f2 = pl.BlockSpec((nb, 1, tt, C), lambda b, j: (b, 0, j, 0))
    b2 = pl.BlockSpec((nb, 1, tt, C), lambda b, j: (b, 1, nT - 1 - j, 0))
    out = jax.ShapeDtypeStruct((B, S, C), BF16)
    return pl.pallas_call(
        functools.partial(_rw_scan_kernel, n_chunks=tt // RW_CHUNK),
        grid=(B // nb, nT),
        in_specs=[f1, f1, f1, f2, f2, f2, b1, b1, b1, b2, b2, b2],
        out_specs=[f1, b1],
        out_shape=[out, out],
        scratch_shapes=[pltpu.VMEM((nb * RW_HEADS, 2 * RW_HEAD_DIM, 2 * RW_HEAD_DIM), F32)],
        compiler_params=_cparams(("parallel", "arbitrary")),
        name="rw_scan",
    )(r, v, kn, lw, kd, bb, r, v, kn, lw, kd, bb)


CV_HALO = 16


def _cv_kernel(p_ref, pp_ref, pn_ref, dw_ref, db_ref, lg_ref, lb_ref, o_ref, u_ref):
    i = pl.program_id(1)
    n = pl.num_programs(1)
    ts = p_ref.shape[1]
    Cc = CV_CHANNELS

    def glu(x):
        x = x.astype(F32)
        return x[:, :Cc] * _sigmoid(x[:, Cc:])

    u_ref[CV_HALO:CV_HALO + ts, :] = glu(p_ref[0])
    u_ref[0:CV_HALO, :] = jnp.where(i > 0, glu(pp_ref[0]), 0.0)
    u_ref[CV_HALO + ts:, :] = jnp.where(i < n - 1, glu(pn_ref[0]), 0.0)
    base = CV_HALO - CONV_TAPS // 2
    ext = ts + 8
    acc = None
    for res in range(8):
        part = None
        for s in range(res, base + CONV_TAPS, 8):
            if s < base:
                continue
            term = u_ref[s - res:s - res + ext, :] * dw_ref[s - base:s - base + 1, :]
            part = term if part is None else part + term
        if part is None:
            continue
        shifted = part[:ts] if res == 0 else pltpu.roll(part, ext - res, 0)[:ts]
        acc = shifted if acc is None else acc + shifted
    acc = acc + db_ref[...]
    mu = jnp.mean(acc, axis=-1, keepdims=True)
    xc = acc - mu
    var = jnp.mean(xc * xc, axis=-1, keepdims=True)
    y = xc * lax.rsqrt(var + CV_LN_EPS) * lg_ref[...] + lb_ref[...]
    o_ref[0] = (y * _sigmoid(y)).astype(o_ref.dtype)


def _cv(p_cv, dw, db, lg, lb, *, ts):
    B, S, _ = p_cv.shape
    nh = S // CV_HALO
    return pl.pallas_call(
        _cv_kernel,
        grid=(B, S // ts),
        in_specs=[
            pl.BlockSpec((1, ts, CV_COLS), lambda b, i: (b, i, 0)),
            pl.BlockSpec((1, CV_HALO, CV_COLS),
                         lambda b, i: (b, jnp.maximum(i * (ts // CV_HALO) - 1, 0), 0)),
            pl.BlockSpec((1, CV_HALO, CV_COLS),
                         lambda b, i: (b, jnp.minimum((i + 1) * (ts // CV_HALO), nh - 1), 0)),
            _wspec(dw), _wspec(db), _wspec(lg), _wspec(lb),
        ],
        out_specs=pl.BlockSpec((1, ts, CV_CHANNELS), lambda b, i: (b, i, 0)),
        out_shape=jax.ShapeDtypeStruct((B, S, CV_CHANNELS), BF16),
        scratch_shapes=[pltpu.VMEM((ts + 2 * CV_HALO, CV_CHANNELS), F32)],
        compiler_params=_cparams(("parallel", "parallel")),
        name="cv_branch",
    )(p_cv, p_cv, p_cv, dw[0], db[0], lg[0], lb[0])


def _mla_prep_kernel(p_ref, cb_ref, s1_ref, s2_ref, qn_ref, wq_ref, wqs_ref, kvn_ref, wkv_ref,
                     gq_ref, gqs_ref, gk_ref, q_o, k_o, v_o):
    p = p_ref[0].astype(F32)
    cb = cb_ref[0]
    s1 = s1_ref[0]
    s2 = s2_ref[0]
    s12 = s1 + s2
    cq = p[:, 0:MLA_Q_RANK]
    ckv = p[:, MLA_Q_RANK:MLA_Q_RANK + MLA_KV_RANK]
    kr = p[:, MLA_Q_RANK + MLA_KV_RANK:MLA_COLS_PAD]
    cqn = _rms(cq, qn_ref[...]).astype(BF16)
    qall = jnp.dot(cqn, wq_ref[...], preferred_element_type=F32)
    qswp = jnp.dot(cqn, wqs_ref[...], preferred_element_type=F32)
    kvall = _mm(_rms(ckv, kvn_ref[...]), wkv_ref[...])
    gq = gq_ref[...]
    gqs = gqs_ref[...]
    gk = gk_ref[...]
    lane = lax.broadcasted_iota(jnp.int32, (1, LANE), 1)
    ones_col = jnp.where(lane == MLA_V, 1.0, 0.0)
    scale = MLA_QK ** -0.5
    HB = MLA_HEADS * LANE

    krot = pltpu.roll(kr, MLA_NOPE, 1)
    kr_ss = jnp.sum(krot * krot, axis=-1, keepdims=True)
    krg = krot * gk
    kr_r = (krg * cb + pltpu.roll(krg, LANE - MLA_ROPE // 2, 1) * s1
            + pltpu.roll(krg, MLA_ROPE // 2, 1) * s2)

    for h in range(MLA_HEADS):
        blk = slice(h * LANE, (h + 1) * LANE)
        qb = qall[:, blk]
        rq = lax.rsqrt(jnp.sum(qb * qb, axis=-1, keepdims=True) * (1.0 / MLA_QK) + NORM_EPS)
        q_o[0, h] = (((qb * gq) * cb + (qswp[:, blk] * gqs) * s12) * (rq * scale)).astype(q_o.dtype)
        kb = kvall[:, blk]
        rk = lax.rsqrt((jnp.sum(kb * kb, axis=-1, keepdims=True) + kr_ss) * (1.0 / MLA_QK) + NORM_EPS)
        k_o[0, h] = ((kb * gk + kr_r) * rk).astype(k_o.dtype)
        v_o[0, h] = (kvall[:, HB + h * LANE:HB + (h + 1) * LANE] + ones_col).astype(v_o.dtype)


def _mla_prep(p_mla, tabs, wts, *, ts):
    B, S, _ = p_mla.shape
    H = MLA_HEADS
    tile = lambda b, i: (b, i, 0)
    tspec = pl.BlockSpec((1, ts, LANE), tile)
    names = ["q_norm", "wq", "wq_swap", "kv_norm", "wkv", "gq", "gq_swap", "gk"]
    hspec = pl.BlockSpec((1, H, ts, LANE), lambda b, i: (b, 0, i, 0))
    hshape = jax.ShapeDtypeStruct((B, H, S, LANE), BF16)
    return pl.pallas_call(
        _mla_prep_kernel,
        grid=(B, S // ts),
        in_specs=[pl.BlockSpec((1, ts, MLA_COLS_PAD), tile), tspec, tspec, tspec]
        + [_wspec(wts[n]) for n in names],
        out_specs=[hspec, hspec, hspec],
        out_shape=[hshape, hshape, hshape],
        compiler_params=_cparams(("parallel", "parallel")),
        name="mla_prep",
    )(p_mla, *tabs, *[wts[n][0] for n in names])


ATTN_SUB = 256


def _attn_kernel(q_ref, k_ref, v_ref, o_ref):
    tq = q_ref.shape[2]
    for r0 in range(0, tq, ATTN_SUB):
        outs = []
        for h in range(2):
            s = _mm_nt(q_ref[0, h, r0:r0 + ATTN_SUB, :], k_ref[0, h])
            m = jnp.max(s, axis=-1, keepdims=True)
            p = jnp.exp(s - m).astype(BF16)
            ov = jnp.dot(p, v_ref[0, h], preferred_element_type=F32)
            outs.append(ov[:, :MLA_V] / ov[:, MLA_V:MLA_V + 1])
        o_ref[0, r0:r0 + ATTN_SUB, :] = jnp.concatenate(outs, axis=1).astype(o_ref.dtype)


def _attn(q, k, v, *, tq):
    B, H, S, _ = q.shape
    return pl.pallas_call(
        _attn_kernel,
        grid=(B, H // 2, S // tq),
        in_specs=[
            pl.BlockSpec((1, 2, tq, LANE), lambda b, h, i: (b, h, i, 0)),
            pl.BlockSpec((1, 2, S, LANE), lambda b, h, i: (b, h, 0, 0)),
            pl.BlockSpec((1, 2, S, LANE), lambda b, h, i: (b, h, 0, 0)),
        ],
        out_specs=pl.BlockSpec((1, tq, 2 * MLA_V), lambda b, h, i: (b, i, h)),
        out_shape=jax.ShapeDtypeStruct((B, S, H * MLA_V), BF16),
        compiler_params=_cparams(("parallel", "parallel", "parallel")),
        name="mla_attn",
    )(q, k, v)


def _merge_kernel(x_ref, of_ref, ob_ref, g_ref, bn_ref, ucv_ref, omla_ref, gate_ref,
                  avg_ref, gng_ref, gnb_ref, wrw_ref, wcv_ref, wmla_ref, wo_ref, out_ref):
    D = D_MODEL
    o = of_ref[...].astype(F32) + ob_ref[...].astype(F32)
    avg = avg_ref[...]
    inv_n = 1.0 / RW_HEAD_DIM
    mu = _mm_x_exact(o, avg) * inv_n
    xc = o - mu
    var = _mm_x_exact(xc * xc, avg) * inv_n
    on = xc * lax.rsqrt(var + RW_GN_EPS) * gng_ref[...] + gnb_ref[...]
    orw = (on + bn_ref[...].astype(F32)) * g_ref[...].astype(F32)
    gate = gate_ref[...].astype(F32)
    merged = (_sigmoid(gate[:, 0:D]) * _mm(orw, wrw_ref[...])
              + _sigmoid(gate[:, D:2 * D]) * jnp.dot(ucv_ref[...], wcv_ref[...], preferred_element_type=F32)
              + _sigmoid(gate[:, 2 * D:3 * D]) * jnp.dot(omla_ref[...], wmla_ref[...],
                                                         preferred_element_type=F32))
    out_ref[...] = x_ref[...] + _mm(merged, wo_ref[...])


def _merge(x2, o_f, o_b, g, bn, ucv, omla, gate, wts, *, tm):
    T, D = x2.shape
    C = RW_WIDTH
    row = lambda n: pl.BlockSpec((tm, n), lambda i: (i, 0))
    names = ["e64", "gn_g", "gn_b", "w_rw", "w_cv", "w_mla", "w_o"]
    return pl.pallas_call(
        _merge_kernel,
        grid=(T // tm,),
        in_specs=[row(D), row(C), row(C), row(C), row(C), row(C), row(C),
                  row(GATE_COLS)] + [_wspec(wts[n]) for n in names],
        out_specs=row(D),
        out_shape=jax.ShapeDtypeStruct((T, D), F32),
        compiler_params=_cparams(("parallel",)),
        name="merge",
    )(x2, o_f, o_b, g, bn, ucv, omla, gate, *[wts[n][0] for n in names])


def _block_diag2(w):
    z = jnp.zeros_like(w[:, 0])
    return jnp.concatenate([jnp.concatenate([w[:, 0], z], axis=2),
                            jnp.concatenate([z, w[:, 1]], axis=2)], axis=1)


def _head_block_cols(w):
    Lw, R, H, width = w.shape
    return jnp.pad(w, ((0, 0), (0, 0), (0, 0), (0, LANE - width))).reshape(Lw, R, H * LANE)


def _rope_tables(positions):
    half = MLA_ROPE // 2
    inv = ROPE_THETA ** (-jnp.arange(0, MLA_ROPE, 2, dtype=F32) / MLA_ROPE)
    ang = positions.astype(F32)[..., None] * inv
    cos, sin = jnp.cos(ang), jnp.sin(ang)
    B, S = positions.shape
    one = jnp.ones((B, S, MLA_NOPE), F32)
    z64 = jnp.zeros((B, S, MLA_NOPE), F32)
    z16 = jnp.zeros((B, S, half), F32)
    pad = jnp.zeros((B, S, LANE - MLA_QK), F32)
    cb = jnp.concatenate([one, cos, cos, pad + 1.0], axis=-1)
    s1 = jnp.concatenate([z64, -sin, z16, pad], axis=-1)
    s2 = jnp.concatenate([z64, z16, sin, pad], axis=-1)
    return cb, s1, s2


def _head_ones():
    hid = jnp.arange(RW_WIDTH) // RW_HEAD_DIM
    return (hid[:, None] == hid[None, :]).astype(BF16)


def _tile_sizes(T, S):
    pick = lambda n, cands: next(c for c in cands if n % c == 0)
    return dict(
        ffn_tm=pick(T, (1024, 512, 256, 128, 64, 32, 16, 8)),
        ffn_tf=pick(D_FF, (256, 128)),
        tok_tm=pick(T, (512, 256, 128, 64, 32, 16, 8)),
        seq_ts=pick(S, (512, 256, 128, 64)),
        scan_tt=pick(S, (512, 256, 128, 64)),
        attn_tq=pick(S, (1024, 512, 256)),
    )


def kernel(x, positions, norm_ffn1, ffn1_w1, ffn1_w3, ffn1_w2, norm_mix, w_in, rw_mu, rw_w0, rw_w_up, rw_a0, rw_a_up, rw_g_up, rw_k_k, rw_k_a, rw_r_k, rw_v0, rw_v_down, rw_v_up, rw_gn_g, rw_gn_b, rw_w_branch, cv_dw_w, cv_dw_b, cv_ln_g, cv_ln_b, cv_w_branch, mla_q_norm, mla_w_uq, mla_kv_norm, mla_w_ukv, mla_qk_q_g, mla_qk_k_g, mla_w_branch, w_o, norm_ffn2, ffn2_w1, ffn2_w3, ffn2_w2):
    B, S, D = x.shape
    T = B * S
    L = w_in.shape[0]
    ts_ = _tile_sizes(T, S)
    tabs = _rope_tables(positions)
    x2 = x.reshape(T, D)

    rows = lambda a: a.reshape(a.shape[0], 1, -1)
    bf = lambda a: a.astype(BF16)
    e64 = (_head_ones(), None)
    c1, c2, c3 = RW_COLS, RW_COLS + CV_COLS, RW_COLS + CV_COLS + MLA_COLS
    pad_last = lambda a, n: jnp.pad(a, [(0, 0)] * (a.ndim - 1) + [(0, n)])
    vdw = jnp.concatenate([jnp.zeros((1, D, RW_VRES_RANK), F32), rw_v_down], axis=0)
    w_cat = bf(jnp.concatenate([
        w_in[:, :, :c2], pad_last(w_in[:, :, c2:c3], MLA_COLS_PAD - MLA_COLS), w_in[:, :, c3:],
        pad_last(vdw, VD_COLS_PAD - RW_VRES_RANK)], axis=2))
    ffn1 = (rows(norm_ffn1), bf(ffn1_w1), bf(ffn1_w3), bf(ffn1_w2))
    ffn2 = (rows(norm_ffn2), bf(ffn2_w1), bf(ffn2_w3), bf(ffn2_w2))
    g_mix = rows(norm_mix)
    rw_all = dict(
        mu=rows(rw_mu), wup=bf(_block_diag2(rw_w_up)), w0=rows(rw_w0), aup=bf(_block_diag2(rw_a_up)),
        a0=rows(rw_a0), gup=bf(rw_g_up), k_k=rows(rw_k_k), k_a=rows(rw_k_a), r_k=rows(rw_r_k))
    rw_res = dict(v0=rows(rw_v0), vup=bf(jnp.pad(rw_v_up, ((0, 0), (0, VD_COLS_PAD - RW_VRES_RANK), (0, 0)))))
    cv_all = (cv_dw_w, rows(cv_dw_b), rows(cv_ln_g), rows(cv_ln_b))
    half = MLA_ROPE // 2
    swap = lambda a: jnp.concatenate(
        [a[..., :MLA_NOPE], a[..., MLA_NOPE + half:], a[..., MLA_NOPE:MLA_NOPE + half]], axis=-1)
    wq = mla_w_uq.reshape(L, MLA_Q_RANK, MLA_HEADS, MLA_QK)
    wkv = mla_w_ukv.reshape(L, MLA_KV_RANK, MLA_HEADS, MLA_NOPE + MLA_V)
    gpad = lambda gv: rows(pad_last(gv, LANE - MLA_QK))
    mla_all = dict(
        q_norm=rows(mla_q_norm), wq=bf(_head_block_cols(wq)), wq_swap=bf(_head_block_cols(swap(wq))),
        kv_norm=rows(mla_kv_norm),
        wkv=bf(jnp.concatenate([_head_block_cols(wkv[..., :MLA_NOPE]), _head_block_cols(wkv[..., MLA_NOPE:])],
                               axis=2)),
        gq=gpad(mla_qk_q_g), gq_swap=gpad(swap(mla_qk_q_g)), gk=gpad(mla_qk_k_g))
    m_all = dict(gn_g=rows(rw_gn_g), gn_b=rows(rw_gn_b), w_rw=bf(rw_w_branch), w_cv=bf(cv_w_branch),
                 w_mla=bf(mla_w_branch), w_o=bf(w_o))

    v_first = None
    for i in range(L):
        at = lambda d, layer=i: {k: (a, layer) for k, a in d.items()}
        x2 = _ffn(x2, *[(a, i) for a in ffn1], tm=ts_["ffn_tm"], tf=ts_["ffn_tf"])
        p_rw, p_cv, p_mla, p_gate, p_vd = _inproj(x2, (g_mix, i), (w_cat, i), tm=ts_["tok_tm"])

        rw_wts = dict(at(rw_all), e64=e64)
        if i > 0:
            rw_wts.update(at(rw_res, i - 1))
        r, v, kn, lw, kd, bb, g, bn = _rw_prep(
            p_rw.reshape(B, S, RW_COLS), p_vd if i > 0 else None, v_first, rw_wts, ts=ts_["seq_ts"])
        if i == 0:
            v_first = v
        o_f, o_b = _rw_scan(r, v, kn, lw, kd, bb, tt=ts_["scan_tt"],
                            nb=2 if B % 2 == 0 else 1)

        ucv = _cv(p_cv.reshape(B, S, CV_COLS), *[(a, i) for a in cv_all], ts=ts_["seq_ts"])

        qh, kh, vh = _mla_prep(p_mla.reshape(B, S, MLA_COLS_PAD), tabs, at(mla_all), ts=ts_["seq_ts"])
        omla = _attn(qh, kh, vh, tq=ts_["attn_tq"])

        x2 = _merge(x2, o_f.reshape(T, -1), o_b.reshape(T, -1), g.reshape(T, -1), bn.reshape(T, -1),
                    ucv.reshape(T, -1), omla.reshape(T, -1), p_gate, dict(at(m_all), e64=e64),
                    tm=ts_["tok_tm"])

        x2 = _ffn(x2, *[(a, i) for a in ffn2], tm=ts_["ffn_tm"], tf=ts_["ffn_tf"])
    return x2.reshape(B, S, D)
```

```python
import functools
import math

import jax
import jax.numpy as jnp
from jax import lax
from jax.experimental import pallas as pl
from jax.experimental.pallas import tpu as pltpu

F32 = jnp.float32
BF16 = jnp.bfloat16

D_MODEL = 1024
D_FF = 2816
NORM_EPS = 1e-6

RW_HEADS = 8
RW_HEAD_DIM = 64
RW_WIDTH = RW_HEADS * RW_HEAD_DIM
RW_DECAY_RANK = 64
RW_ICL_RANK = 64
RW_VRES_RANK = 32
RW_GATE_RANK = 128
RW_GN_EPS = 64e-5
RW_COLS = 3 * RW_WIDTH + 2 * RW_DECAY_RANK + 2 * RW_ICL_RANK + RW_GATE_RANK

CV_CHANNELS = 512
CONV_TAPS = 31
CV_LN_EPS = 1e-5
CV_COLS = 2 * CV_CHANNELS

MLA_HEADS = 8
MLA_Q_RANK = 384
MLA_KV_RANK = 256
MLA_NOPE = 64
MLA_ROPE = 32
MLA_QK = MLA_NOPE + MLA_ROPE
MLA_V = 64
ROPE_THETA = 10000.0
MLA_COLS = MLA_Q_RANK + MLA_KV_RANK + MLA_ROPE
MLA_COLS_PAD = 768
GATE_COLS = 3 * D_MODEL
VD_COLS_PAD = 128

LANE = 128
VMEM_LIMIT = 56 * 1024 * 1024

RW_CHUNK = 64


def _cparams(sem):
    return pltpu.CompilerParams(dimension_semantics=sem, vmem_limit_bytes=VMEM_LIMIT)


def _wspec(w, **kw):
    arr, layer = w
    if layer is None:
        n = arr.ndim
        return pl.BlockSpec(arr.shape, lambda *_: (0,) * n, **kw)
    n = arr.ndim - 1
    return pl.BlockSpec((None,) + arr.shape[1:], lambda *_: (layer,) + (0,) * n, **kw)


def _mm(a, b):
    return jnp.dot(a.astype(BF16), b.astype(BF16), preferred_element_type=F32)


def _mm_nt(a, b):
    return lax.dot_general(a.astype(BF16), b.astype(BF16), (((1,), (1,)), ((), ())),
                           preferred_element_type=F32)


def _mm_tn(a, b):
    return lax.dot_general(a.astype(BF16), b.astype(BF16), (((0,), (0,)), ((), ())),
                           preferred_element_type=F32)


def _split3(x):
    hi = x.astype(BF16)
    r1 = x - hi.astype(F32)
    mid = r1.astype(BF16)
    lo = (r1 - mid.astype(F32)).astype(BF16)
    return hi, mid, lo


def _mm_x_exact(x, m):
    hi = x.astype(BF16)
    lo = (x - hi.astype(F32)).astype(BF16)
    return jnp.dot(hi, m, preferred_element_type=F32) + jnp.dot(lo, m, preferred_element_type=F32)


def _mm_exact_x(m, x):
    hi, mid, lo = _split3(x)
    return (jnp.dot(m, hi, preferred_element_type=F32) + jnp.dot(m, mid, preferred_element_type=F32)
            + jnp.dot(m, lo, preferred_element_type=F32))


def _sigmoid(x):
    return 1.0 / (1.0 + jnp.exp(-x))


def _rms(x, g, eps=NORM_EPS):
    return x * lax.rsqrt(jnp.mean(x * x, axis=-1, keepdims=True) + eps) * g


def _ffn_kernel(x_ref, g_ref, w1_ref, w3_ref, w2_ref, o_ref, h_ref, *, tf):
    x = x_ref[...]
    h_ref[...] = _rms(x, g_ref[...]).astype(BF16)
    o_ref[...] = x
    n_slab = w1_ref.shape[1] // tf

    def slab(j, carry):
        c = pl.ds(pl.multiple_of(j * tf, tf), tf)
        h = h_ref[...]
        a = jnp.dot(h, w1_ref[:, c], preferred_element_type=F32)
        b = jnp.dot(h, w3_ref[:, c], preferred_element_type=F32)
        u = (a * _sigmoid(a) * b).astype(BF16)
        o_ref[...] += 0.5 * jnp.dot(u, w2_ref[c, :], preferred_element_type=F32)
        return carry

    lax.fori_loop(0, n_slab, slab, 0)


def _ffn(x2, g, w1, w3, w2, *, tm, tf):
    T, D = x2.shape
    once = dict(pipeline_mode=pl.Buffered(1))
    return pl.pallas_call(
        functools.partial(_ffn_kernel, tf=tf),
        grid=(T // tm,),
        in_specs=[pl.BlockSpec((tm, D), lambda i: (i, 0)), _wspec(g),
                  _wspec(w1, **once), _wspec(w3, **once), _wspec(w2, **once)],
        out_specs=pl.BlockSpec((tm, D), lambda i: (i, 0)),
        out_shape=jax.ShapeDtypeStruct((T, D), F32),
        scratch_shapes=[pltpu.VMEM((tm, D), BF16)],
        compiler_params=_cparams(("parallel",)),
        name="ffn",
    )(x2, g[0], w1[0], w3[0], w2[0])


def _inproj_kernel(x_ref, g_ref, w_ref, rw_ref, cv_ref, mla_ref, gate_ref, vd_ref):
    h = _rms(x_ref[...], g_ref[...]).astype(BF16)
    c = 0
    for ref in (rw_ref, cv_ref, mla_ref, gate_ref, vd_ref):
        n = ref.shape[1]
        ref[...] = jnp.dot(h, w_ref[:, c:c + n], preferred_element_type=F32).astype(ref.dtype)
        c += n


def _inproj(x2, g, w_cat, *, tm):
    T, D = x2.shape
    widths = (RW_COLS, CV_COLS, MLA_COLS_PAD, GATE_COLS, VD_COLS_PAD)
    assert w_cat[0].shape[-1] == sum(widths)
    return pl.pallas_call(
        _inproj_kernel,
        grid=(T // tm,),
        in_specs=[pl.BlockSpec((tm, D), lambda i: (i, 0)), _wspec(g), _wspec(w_cat)],
        out_specs=[pl.BlockSpec((tm, n), lambda i: (i, 0)) for n in widths],
        out_shape=[jax.ShapeDtypeStruct((T, n), dt)
                   for n, dt in zip(widths, (F32, BF16, BF16, BF16, F32))],
        compiler_params=_cparams(("parallel",)),
        name="inproj",
    )(x2, g[0], w_cat[0])


def _rw_prep_kernel(*refs, has_vres):
    if has_vres:
        (p_ref, pp_ref, pn_ref, vd_ref, vf_ref, mu_ref, wup_ref, w0_ref, aup_ref, a0_ref, gup_ref,
         kk_ref, ka_ref, rk_ref, e_ref, v0_ref, vup_ref,
         r_o, v_o, kn_o, lw_o, kd_o, ic_o, g_o, bn_o) = refs
    else:
        (p_ref, pp_ref, pn_ref, mu_ref, wup_ref, w0_ref, aup_ref, a0_ref, gup_ref,
         kk_ref, ka_ref, rk_ref, e_ref,
         r_o, v_o, kn_o, lw_o, kd_o, ic_o, g_o, bn_o) = refs
    i = pl.program_id(1)
    n = pl.num_programs(1)
    C = RW_WIDTH
    p = p_ref[0]
    ts = p.shape[0]
    prev_row = jnp.where(i > 0, pp_ref[0, 7:8, :], 0.0)
    next_row = jnp.where(i < n - 1, pn_ref[0, 0:1, :], 0.0)
    rows = lax.broadcasted_iota(jnp.int32, (ts, 1), 0)
    prev = jnp.where(rows == 0, prev_row, pltpu.roll(p, 1, 0))
    nxt = jnp.where(rows == ts - 1, next_row, pltpu.roll(p, ts - 1, 0))
    pm = p + (0.5 * (prev + nxt) - p) * mu_ref[...]

    r = pm[:, 0:C]
    k = pm[:, C:2 * C]
    v = pm[:, 2 * C:3 * C]
    o1 = 3 * C
    wd = jnp.tanh(pm[:, o1:o1 + 2 * RW_DECAY_RANK])
    o2 = o1 + 2 * RW_DECAY_RANK
    ad = pm[:, o2:o2 + 2 * RW_ICL_RANK]
    o3 = o2 + 2 * RW_ICL_RANK
    gd = pm[:, o3:o3 + RW_GATE_RANK]

    wl = w0_ref[...] + _mm(wd, wup_ref[...])
    lw = (-math.exp(-0.5)) * _sigmoid(wl)
    icl = _sigmoid(a0_ref[...] + _mm(ad, aup_ref[...]))
    g = _mm(_sigmoid(gd), gup_ref[...])

    if has_vres:
        nu = _sigmoid(v0_ref[...] + _mm(vd_ref[...], vup_ref[...]))
        v = v + (vf_ref[0] - v) * nu

    e = e_ref[...]
    kkr = k * kk_ref[...]
    ss = _mm_x_exact(kkr * kkr, e)
    kn = kkr * lax.rsqrt(jnp.maximum(ss, 1e-24))

    ka = ka_ref[...]
    kd0 = k * (1.0 + (icl[:, 0:C] - 1.0) * ka)
    kd1 = k * (1.0 + (icl[:, C:2 * C] - 1.0) * ka)
    kb = 0.5 * (kd0 + kd1)
    sb = _mm_x_exact(r * kb * rk_ref[...], e)

    r_o[0] = r.astype(r_o.dtype)
    v_o[0] = v.astype(v_o.dtype)
    kn_o[0] = kn.astype(kn_o.dtype)
    lw_o[0, 0] = lw[:, 0:C]
    lw_o[0, 1] = lw[:, C:2 * C]
    kd_o[0, 0] = kd0.astype(kd_o.dtype)
    kd_o[0, 1] = kd1.astype(kd_o.dtype)
    ic_o[0, 0] = (kn * icl[:, 0:C]).astype(ic_o.dtype)
    ic_o[0, 1] = (kn * icl[:, C:2 * C]).astype(ic_o.dtype)
    g_o[0] = g.astype(g_o.dtype)
    bn_o[0] = (sb * v).astype(bn_o.dtype)


def _rw_prep(p_rw, vd, v_first, wts, *, ts):
    B, S, _ = p_rw.shape
    C = RW_WIDTH
    has_vres = vd is not None
    nblk8 = S // 8
    tile = lambda b, i: (b, i, 0)
    in_arrays = [p_rw, p_rw, p_rw]
    in_specs = [
        pl.BlockSpec((1, ts, RW_COLS), tile),
        pl.BlockSpec((1, 8, RW_COLS), lambda b, i: (b, jnp.maximum(i * (ts // 8) - 1, 0), 0)),
        pl.BlockSpec((1, 8, RW_COLS), lambda b, i: (b, jnp.minimum((i + 1) * (ts // 8), nblk8 - 1), 0)),
    ]
    if has_vres:
        in_arrays += [vd, v_first]
        in_specs += [pl.BlockSpec((ts, VD_COLS_PAD), lambda b, i: (b * (S // ts) + i, 0)),
                     pl.BlockSpec((1, ts, C), tile)]
    names = ["mu", "wup", "w0", "aup", "a0", "gup", "k_k", "k_a", "r_k", "e64"]
    if has_vres:
        names += ["v0", "vup"]
    for nme in names:
        in_arrays.append(wts[nme][0])
        in_specs.append(_wspec(wts[nme]))
    one = jax.ShapeDtypeStruct((B, S, C), BF16)
    two = jax.ShapeDtypeStruct((B, 2, S, C), BF16)
    two_f32 = jax.ShapeDtypeStruct((B, 2, S, C), F32)
    ospec1 = pl.BlockSpec((1, ts, C), tile)
    ospec2 = pl.BlockSpec((1, 2, ts, C), lambda b, i: (b, 0, i, 0))
    return pl.pallas_call(
        functools.partial(_rw_prep_kernel, has_vres=has_vres),
        grid=(B, S // ts),
        in_specs=in_specs,
        out_specs=[ospec1, ospec1, ospec1, ospec2, ospec2, ospec2, ospec1, ospec1],
        out_shape=[one, one, one, two_f32, two, two, one, one],
        compiler_params=_cparams(("parallel", "parallel")),
        name="rw_prep",
    )(*in_arrays)


def _rw_scan_kernel(rf_ref, vf_ref, knf_ref, lwf_ref, kdf_ref, bbf_ref,
                    rb_ref, vb_ref, knb_ref, lwb_ref, kdb_ref, bbb_ref,
                    of_ref, ob_ref, z_ref, *, n_chunks):
    j = pl.program_id(1)
    C = RW_CHUNK
    N = RW_HEAD_DIM
    P = RW_HEADS // 2
    W = 2 * N

    @pl.when(j == 0)
    def _():
        z_ref[...] = jnp.zeros_like(z_ref)

    row2 = lax.broadcasted_iota(jnp.int32, (2 * C, W), 0)
    col2 = lax.broadcasted_iota(jnp.int32, (2 * C, W), 1)
    order = (row2 % C) - (col2 % C)
    need = jnp.where(row2 >= C, 0, 1)
    gmask = (order >= need, -order >= need)
    left2 = (col2 < N) == (row2 < C)
    row1 = lax.broadcasted_iota(jnp.int32, (C, W), 0)
    col1 = lax.broadcasted_iota(jnp.int32, (C, W), 1)
    eye = jnp.where(col1 % N == row1, 1.0, 0.0)
    t1 = lax.broadcasted_iota(jnp.int32, (C, C), 0)
    i1 = lax.broadcasted_iota(jnp.int32, (C, C), 1)
    tri = (jnp.where(i1 <= t1, 1.0, 0.0).astype(BF16), jnp.where(i1 >= t1, 1.0, 0.0).astype(BF16))
    in_refs = ((rf_ref, vf_ref, knf_ref, lwf_ref, kdf_ref, bbf_ref),
               (rb_ref, vb_ref, knb_ref, lwb_ref, kdb_ref, bbb_ref))
    out_refs = (of_ref, ob_ref)
    slabs = [(bi, d) for bi in range(rf_ref.shape[0]) for d in range(2)]

    def bd(x):
        xb = x.astype(BF16)
        return jnp.where(left2, jnp.concatenate([xb, xb], axis=0), jnp.zeros((), BF16))

    def chunk(cc, carry):
        rows_d = (pl.ds(pl.multiple_of(cc * C, C), C),
                  pl.ds(pl.multiple_of((n_chunks - 1 - cc) * C, C), C))
        wide = []
        for bi, d in slabs:
            r_ref, v_ref, kn_ref, lw_ref, kd_ref, bb_ref = in_refs[d]
            rows = rows_d[d]
            r = r_ref[bi, rows, :].astype(F32)
            v = v_ref[bi, rows, :]
            kn = kn_ref[bi, rows, :].astype(F32)
            lw = lw_ref[bi, 0, rows, :]
            kd = kd_ref[bi, 0, rows, :].astype(F32)
            b = bb_ref[bi, 0, rows, :].astype(F32)
            lc = _mm_exact_x(tri[d], lw)
            ltot = jnp.sum(lw, axis=0, keepdims=True)
            e_nlc = jnp.exp(-lc)
            e_t = jnp.exp(ltot - lc)
            wide.append(dict(
                at=(-kn * jnp.exp(lc - lw)).astype(BF16), rt=(r * jnp.exp(lc)).astype(BF16),
                bt=(b * e_nlc).astype(BF16), kt=(kd * e_nlc).astype(BF16),
                bh=(b * e_t).astype(BF16), kh=(kd * e_t).astype(BF16),
                v=v.astype(BF16), gam=jnp.exp(ltot)))
        chains = [(s, p) for s in range(len(slabs)) for p in range(P)]
        ps = lambda s, p, name: wide[s][name][:, p * W:(p + 1) * W]
        mm = lambda a, bmat: jnp.dot(a.astype(BF16), bmat, preferred_element_type=F32)
        dmask = lambda s: gmask[slabs[s][1]]

        x = [jnp.concatenate([ps(d, p, "at"), ps(d, p, "rt")], axis=0) for d, p in chains]
        gb = [jnp.where(dmask(d), _mm_nt(xi, bd(ps(d, p, "bt"))), 0.0) for xi, (d, p) in zip(x, chains)]
        gk = [jnp.where(dmask(d), _mm_nt(xi, bd(ps(d, p, "kt"))), 0.0) for xi, (d, p) in zip(x, chains)]
        l1 = [g[:C] for g in gb]
        tinv = [eye + l for l in l1]
        lp = [mm(l, bd(l)) for l in l1]
        n_lvl = int(math.log2(C))
        for lvl in range(2, n_lvl):
            both = [mm(jnp.concatenate([l, t], axis=0), bd(l)) for l, t in zip(lp, tinv)]
            tinv = [t + bo[C:] for t, bo in zip(tinv, both)]
            lp = [bo[:C] for bo in both]
        tinv = [t + mm(t, bd(l)) for t, l in zip(tinv, lp)]
        ah = [mm(t, bd(ps(d, p, "at"))) for t, (d, p) in zip(tinv, chains)]
        w = [mm(g[:C], bd(ps(d, p, "v"))) for g, (d, p) in zip(gk, chains)]
        vh = [mm(t, bd(wi)) for t, wi in zip(tinv, w)]
        z = [z_ref[n] for n in range(len(chains))]
        az = [_mm_nt(jnp.concatenate([a.astype(BF16), ps(d, p, "rt")], axis=0), zi)
              for a, zi, (d, p) in zip(ah, z, chains)]
        u = [(azi[:C] + vhi).astype(BF16) for azi, vhi in zip(az, vh)]
        o = [azi[C:] + mm(g1[C:], bd(ui)) + mm(g2[C:], bd(ps(d, p, "v")))
             for azi, g1, g2, ui, (d, p) in zip(az, gb, gk, u, chains)]
        znew = [zi * wide[d]["gam"][:, p * W:(p + 1) * W]
                + jnp.where(left2, _mm_tn(jnp.concatenate([ui, ps(d, p, "v")], axis=0),
                                          jnp.concatenate([ps(d, p, "bh"), ps(d, p, "kh")], axis=0)), 0.0)
                for zi, ui, (d, p) in zip(z, u, chains)]
        z_ref[...] = jnp.stack(znew, axis=0)
        for s, (bi, d) in enumerate(slabs):
            out_refs[d][bi, rows_d[d], :] = jnp.concatenate(o[s * P:(s + 1) * P], axis=1).astype(
                out_refs[d].dtype)
        return carry

    lax.fori_loop(0, n_chunks, chunk, 0)


def _rw_scan(r, v, kn, lw, kd, bb, *, tt, nb):
    B, S, C = r.shape
    nT = S // tt
    f1 = pl.BlockSpec((nb, tt, C), lambda b, j: (b, j, 0))
    b1 = pl.BlockSpec((nb, tt, C), lambda b, j: (b, nT - 1 - j, 0))
    f2 = pl.BlockSpec((nb, 1, tt, C), lambda b, j: (b, 0, j, 0))
    b2 = pl.BlockSpec((nb, 1, tt, C), lambda b, j: (b, 1, nT - 1 - j, 0))
    out = jax.ShapeDtypeStruct((B, S, C), BF16)
    return pl.pallas_call(
        functools.partial(_rw_scan_kernel, n_chunks=tt // RW_CHUNK),
        grid=(B // nb, nT),
        in_specs=[f1, f1, f1, f2, f2, f2, b1, b1, b1, b2, b2, b2],
        out_specs=[f1, b1],
        out_shape=[out, out],
        scratch_shapes=[pltpu.VMEM((nb * RW_HEADS, 2 * RW_HEAD_DIM, 2 * RW_HEAD_DIM), F32)],
        compiler_params=_cparams(("parallel", "arbitrary")),
        name="rw_scan",
    )(r, v, kn, lw, kd, bb, r, v, kn, lw, kd, bb)


CV_HALO = 16


def _cv_kernel(p_ref, pp_ref, pn_ref, dw_ref, db_ref, lg_ref, lb_ref, o_ref, u_ref):
    i = pl.program_id(1)
    n = pl.num_programs(1)
    ts = p_ref.shape[1]
    Cc = CV_CHANNELS

    def glu(x):
        x = x.astype(F32)
        return x[:, :Cc] * _sigmoid(x[:, Cc:])

    u_ref[CV_HALO:CV_HALO + ts, :] = glu(p_ref[0])
    u_ref[0:CV_HALO, :] = jnp.where(i > 0, glu(pp_ref[0]), 0.0)
    u_ref[CV_HALO + ts:, :] = jnp.where(i < n - 1, glu(pn_ref[0]), 0.0)
    base = CV_HALO - CONV_TAPS // 2
    ext = ts + 8
    acc = None
    for res in range(8):
        part = None
        for s in range(res, base + CONV_TAPS, 8):
            if s < base:
                continue
            term = u_ref[s - res:s - res + ext, :] * dw_ref[s - base:s - base + 1, :]
            part = term if part is None else part + term
        if part is None:
            continue
        shifted = part[:ts] if res == 0 else pltpu.roll(part, ext - res, 0)[:ts]
        acc = shifted if acc is None else acc + shifted
    acc = acc + db_ref[...]
    mu = jnp.mean(acc, axis=-1, keepdims=True)
    xc = acc - mu
    var = jnp.mean(xc * xc, axis=-1, keepdims=True)
    y = xc * lax.rsqrt(var + CV_LN_EPS) * lg_ref[...] + lb_ref[...]
    o_ref[0] = (y * _sigmoid(y)).astype(o_ref.dtype)


def _cv(p_cv, dw, db, lg, lb, *, ts):
    B, S, _ = p_cv.shape
    nh = S // CV_HALO
    return pl.pallas_call(
        _cv_kernel,
        grid=(B, S // ts),
        in_specs=[
            pl.BlockSpec((1, ts, CV_COLS), lambda b, i: (b, i, 0)),
            pl.BlockSpec((1, CV_HALO, CV_COLS),
                         lambda b, i: (b, jnp.maximum(i * (ts // CV_HALO) - 1, 0), 0)),
            pl.BlockSpec((1, CV_HALO, CV_COLS),
                         lambda b, i: (b, jnp.minimum((i + 1) * (ts // CV_HALO), nh - 1), 0)),
            _wspec(dw), _wspec(db), _wspec(lg), _wspec(lb),
        ],
        out_specs=pl.BlockSpec((1, ts, CV_CHANNELS), lambda b, i: (b, i, 0)),
        out_shape=jax.ShapeDtypeStruct((B, S, CV_CHANNELS), BF16),
        scratch_shapes=[pltpu.VMEM((ts + 2 * CV_HALO, CV_CHANNELS), F32)],
        compiler_params=_cparams(("parallel", "parallel")),
        name="cv_branch",
    )(p_cv, p_cv, p_cv, dw[0], db[0], lg[0], lb[0])


def _mla_prep_kernel(p_ref, cb_ref, s1_ref, s2_ref, qn_ref, wq_ref, wqs_ref, kvn_ref, wkv_ref,
                     gq_ref, gqs_ref, gk_ref, q_o, k_o, v_o):
    p = p_ref[0].astype(F32)
    cb = cb_ref[0]
    s1 = s1_ref[0]
    s2 = s2_ref[0]
    s12 = s1 + s2
    cq = p[:, 0:MLA_Q_RANK]
    ckv = p[:, MLA_Q_RANK:MLA_Q_RANK + MLA_KV_RANK]
    kr = p[:, MLA_Q_RANK + MLA_KV_RANK:MLA_COLS_PAD]
    cqn = _rms(cq, qn_ref[...]).astype(BF16)
    qall = jnp.dot(cqn, wq_ref[...], preferred_element_type=F32)
    qswp = jnp.dot(cqn, wqs_ref[...], preferred_element_type=F32)
    kvall = _mm(_rms(ckv, kvn_ref[...]), wkv_ref[...])
    gq = gq_ref[...]
    gqs = gqs_ref[...]
    gk = gk_ref[...]
    lane = lax.broadcasted_iota(jnp.int32, (1, LANE), 1)
    ones_col = jnp.where(lane == MLA_V, 1.0, 0.0)
    scale = MLA_QK ** -0.5 * math.log2(math.e)
    HB = MLA_HEADS * LANE

    krot = pltpu.roll(kr, MLA_NOPE, 1)
    kr_ss = jnp.sum(krot * krot, axis=-1, keepdims=True)
    krg = krot * gk
    kr_r = (krg * cb + pltpu.roll(krg, LANE - MLA_ROPE // 2, 1) * s1
            + pltpu.roll(krg, MLA_ROPE // 2, 1) * s2)

    for h in range(MLA_HEADS):
        blk = slice(h * LANE, (h + 1) * LANE)
        qb = qall[:, blk]
        rq = lax.rsqrt(jnp.sum(qb * qb, axis=-1, keepdims=True) * (1.0 / MLA_QK) + NORM_EPS)
        q_o[0, h] = (((qb * gq) * cb + (qswp[:, blk] * gqs) * s12) * (rq * scale)).astype(q_o.dtype)
        kb = kvall[:, blk]
        rk = lax.rsqrt((jnp.sum(kb * kb, axis=-1, keepdims=True) + kr_ss) * (1.0 / MLA_QK) + NORM_EPS)
        k_o[0, h] = ((kb * gk + kr_r) * rk).astype(k_o.dtype)
        v_o[0, h] = (kvall[:, HB + h * LANE:HB + (h + 1) * LANE] + ones_col).astype(v_o.dtype)


def _mla_prep(p_mla, tabs, wts, *, ts):
    B, S, _ = p_mla.shape
    H = MLA_HEADS
    tile = lambda b, i: (b, i, 0)
    tspec = pl.BlockSpec((1, ts, LANE), tile)
    names = ["q_norm", "wq", "wq_swap", "kv_norm", "wkv", "gq", "gq_swap", "gk"]
    hspec = pl.BlockSpec((1, H, ts, LANE), lambda b, i: (b, 0, i, 0))
    hshape = jax.ShapeDtypeStruct((B, H, S, LANE), BF16)
    return pl.pallas_call(
        _mla_prep_kernel,
        grid=(B, S // ts),
        in_specs=[pl.BlockSpec((1, ts, MLA_COLS_PAD), tile), tspec, tspec, tspec]
        + [_wspec(wts[n]) for n in names],
        out_specs=[hspec, hspec, hspec],
        out_shape=[hshape, hshape, hshape],
        compiler_params=_cparams(("parallel", "parallel")),
        name="mla_prep",
    )(p_mla, *tabs, *[wts[n][0] for n in names])


ATTN_SUB = 512


def _attn_kernel(q_ref, k_ref, v_ref, o_ref):
    tq = q_ref.shape[2]
    items = [(r0, h) for r0 in range(0, tq, ATTN_SUB) for h in range(2)]
    scores = lambda r0, h: _mm_nt(q_ref[0, h, r0:r0 + ATTN_SUB, :], k_ref[0, h])
    s = scores(*items[0])
    outs = []
    for n, (r0, h) in enumerate(items):
        s_next = scores(*items[n + 1]) if n + 1 < len(items) else None
        m = jnp.max(s, axis=-1, keepdims=True)
        p = jnp.exp2(s - m).astype(BF16)
        ov = jnp.dot(p, v_ref[0, h], preferred_element_type=F32)
        outs.append(ov[:, :MLA_V] / ov[:, MLA_V:MLA_V + 1])
        if h == 1:
            o_ref[0, r0:r0 + ATTN_SUB, :] = jnp.concatenate(outs, axis=1).astype(o_ref.dtype)
            outs = []
        s = s_next


def _attn(q, k, v, *, tq):
    B, H, S, _ = q.shape
    return pl.pallas_call(
        _attn_kernel,
        grid=(B, H // 2, S // tq),
        in_specs=[
            pl.BlockSpec((1, 2, tq, LANE), lambda b, h, i: (b, h, i, 0)),
            pl.BlockSpec((1, 2, S, LANE), lambda b, h, i: (b, h, 0, 0)),
            pl.BlockSpec((1, 2, S, LANE), lambda b, h, i: (b, h, 0, 0)),
        ],
        out_specs=pl.BlockSpec((1, tq, 2 * MLA_V), lambda b, h, i: (b, i, h)),
        out_shape=jax.ShapeDtypeStruct((B, S, H * MLA_V), BF16),
        compiler_params=_cparams(("parallel", "parallel", "parallel")),
        name="mla_attn",
    )(q, k, v)


def _merge_kernel(x_ref, of_ref, ob_ref, g_ref, bn_ref, ucv_ref, omla_ref, gate_ref,
                  avg_ref, gng_ref, gnb_ref, wrw_ref, wcv_ref, wmla_ref, wo_ref, out_ref):
    D = D_MODEL
    o = of_ref[...].astype(F32) + ob_ref[...].astype(F32)
    avg = avg_ref[...]
    inv_n = 1.0 / RW_HEAD_DIM
    mu = _mm_x_exact(o, avg) * inv_n
    xc = o - mu
    var = _mm_x_exact(xc * xc, avg) * inv_n
    on = xc * lax.rsqrt(var + RW_GN_EPS) * gng_ref[...] + gnb_ref[...]
    orw = (on + bn_ref[...].astype(F32)) * g_ref[...].astype(F32)
    gate = gate_ref[...].astype(F32)
    merged = (_sigmoid(gate[:, 0:D]) * _mm(orw, wrw_ref[...])
              + _sigmoid(gate[:, D:2 * D]) * jnp.dot(ucv_ref[...], wcv_ref[...], preferred_element_type=F32)
              + _sigmoid(gate[:, 2 * D:3 * D]) * jnp.dot(omla_ref[...], wmla_ref[...],
                                                         preferred_element_type=F32))
    out_ref[...] = x_ref[...] + _mm(merged, wo_ref[...])


def _merge(x2, o_f, o_b, g, bn, ucv, omla, gate, wts, *, tm):
    T, D = x2.shape
    C = RW_WIDTH
    row = lambda n: pl.BlockSpec((tm, n), lambda i: (i, 0))
    names = ["e64", "gn_g", "gn_b", "w_rw", "w_cv", "w_mla", "w_o"]
    return pl.pallas_call(
        _merge_kernel,
        grid=(T // tm,),
        in_specs=[row(D), row(C), row(C), row(C), row(C), row(C), row(C),
                  row(GATE_COLS)] + [_wspec(wts[n]) for n in names],
        out_specs=row(D),
        out_shape=jax.ShapeDtypeStruct((T, D), F32),
        compiler_params=_cparams(("parallel",)),
        name="merge",
    )(x2, o_f, o_b, g, bn, ucv, omla, gate, *[wts[n][0] for n in names])


def _block_diag2(w):
    z = jnp.zeros_like(w[:, 0])
    return jnp.concatenate([jnp.concatenate([w[:, 0], z], axis=2),
                            jnp.concatenate([z, w[:, 1]], axis=2)], axis=1)


def _head_block_cols(w):
    Lw, R, H, width = w.shape
    return jnp.pad(w, ((0, 0), (0, 0), (0, 0), (0, LANE - width))).reshape(Lw, R, H * LANE)


def _rope_tables(positions):
    half = MLA_ROPE // 2
    inv = ROPE_THETA ** (-jnp.arange(0, MLA_ROPE, 2, dtype=F32) / MLA_ROPE)
    ang = positions.astype(F32)[..., None] * inv
    cos, sin = jnp.cos(ang), jnp.sin(ang)
    B, S = positions.shape
    one = jnp.ones((B, S, MLA_NOPE), F32)
    z64 = jnp.zeros((B, S, MLA_NOPE), F32)
    z16 = jnp.zeros((B, S, half), F32)
    pad = jnp.zeros((B, S, LANE - MLA_QK), F32)
    cb = jnp.concatenate([one, cos, cos, pad + 1.0], axis=-1)
    s1 = jnp.concatenate([z64, -sin, z16, pad], axis=-1)
    s2 = jnp.concatenate([z64, z16, sin, pad], axis=-1)
    return cb, s1, s2


def _head_ones():
    hid = jnp.arange(RW_WIDTH) // RW_HEAD_DIM
    return (hid[:, None] == hid[None, :]).astype(BF16)


def _tile_sizes(T, S):
    pick = lambda n, cands: next(c for c in cands if n % c == 0)
    return dict(
        ffn_tm=pick(T, (1024, 512, 256, 128, 64, 32, 16, 8)),
        ffn_tf=pick(D_FF, (256, 128)),
        tok_tm=pick(T, (512, 256, 128, 64, 32, 16, 8)),
        seq_ts=pick(S, (512, 256, 128, 64)),
        scan_tt=pick(S, (512, 256, 128, 64)),
        attn_tq=pick(S, (1024, 512, 256)),
    )


def kernel(x, positions, norm_ffn1, ffn1_w1, ffn1_w3, ffn1_w2, norm_mix, w_in, rw_mu, rw_w0, rw_w_up, rw_a0, rw_a_up, rw_g_up, rw_k_k, rw_k_a, rw_r_k, rw_v0, rw_v_down, rw_v_up, rw_gn_g, rw_gn_b, rw_w_branch, cv_dw_w, cv_dw_b, cv_ln_g, cv_ln_b, cv_w_branch, mla_q_norm, mla_w_uq, mla_kv_norm, mla_w_ukv, mla_qk_q_g, mla_qk_k_g, mla_w_branch, w_o, norm_ffn2, ffn2_w1, ffn2_w3, ffn2_w2):
    B, S, D = x.shape
    T = B * S
    L = w_in.shape[0]
    ts_ = _tile_sizes(T, S)
    tabs = _rope_tables(positions)
    x2 = x.reshape(T, D)

    rows = lambda a: a.reshape(a.shape[0], 1, -1)
    bf = lambda a: a.astype(BF16)
    e64 = (_head_ones(), None)
    c1, c2, c3 = RW_COLS, RW_COLS + CV_COLS, RW_COLS + CV_COLS + MLA_COLS
    pad_last = lambda a, n: jnp.pad(a, [(0, 0)] * (a.ndim - 1) + [(0, n)])
    vdw = jnp.concatenate([jnp.zeros((1, D, RW_VRES_RANK), F32), rw_v_down], axis=0)
    w_cat = bf(jnp.concatenate([
        w_in[:, :, :c2], pad_last(w_in[:, :, c2:c3], MLA_COLS_PAD - MLA_COLS), w_in[:, :, c3:],
        pad_last(vdw, VD_COLS_PAD - RW_VRES_RANK)], axis=2))
    ffn1 = (rows(norm_ffn1), bf(ffn1_w1), bf(ffn1_w3), bf(ffn1_w2))
    ffn2 = (rows(norm_ffn2), bf(ffn2_w1), bf(ffn2_w3), bf(ffn2_w2))
    g_mix = rows(norm_mix)
    rw_all = dict(
        mu=rows(rw_mu), wup=bf(_block_diag2(rw_w_up)), w0=rows(rw_w0), aup=bf(_block_diag2(rw_a_up)),
        a0=rows(rw_a0), gup=bf(rw_g_up), k_k=rows(rw_k_k), k_a=rows(rw_k_a), r_k=rows(rw_r_k))
    rw_res = dict(v0=rows(rw_v0), vup=bf(jnp.pad(rw_v_up, ((0, 0), (0, VD_COLS_PAD - RW_VRES_RANK), (0, 0)))))
    cv_all = (cv_dw_w, rows(cv_dw_b), rows(cv_ln_g), rows(cv_ln_b))
    half = MLA_ROPE // 2
    swap = lambda a: jnp.concatenate(
        [a[..., :MLA_NOPE], a[..., MLA_NOPE + half:], a[..., MLA_NOPE:MLA_NOPE + half]], axis=-1)
    wq = mla_w_uq.reshape(L, MLA_Q_RANK, MLA_HEADS, MLA_QK)
    wkv = mla_w_ukv.reshape(L, MLA_KV_RANK, MLA_HEADS, MLA_NOPE + MLA_V)
    gpad = lambda gv: rows(pad_last(gv, LANE - MLA_QK))
    mla_all = dict(
        q_norm=rows(mla_q_norm), wq=bf(_head_block_cols(wq)), wq_swap=bf(_head_block_cols(swap(wq))),
        kv_norm=rows(mla_kv_norm),
        wkv=bf(jnp.concatenate([_head_block_cols(wkv[..., :MLA_NOPE]), _head_block_cols(wkv[..., MLA_NOPE:])],
                               axis=2)),
        gq=gpad(mla_qk_q_g), gq_swap=gpad(swap(mla_qk_q_g)), gk=gpad(mla_qk_k_g))
    m_all = dict(gn_g=rows(rw_gn_g), gn_b=rows(rw_gn_b), w_rw=bf(rw_w_branch), w_cv=bf(cv_w_branch),
                 w_mla=bf(mla_w_branch), w_o=bf(w_o))

    v_first = None
    for i in range(L):
        at = lambda d, layer=i: {k: (a, layer) for k, a in d.items()}
        x2 = _ffn(x2, *[(a, i) for a in ffn1], tm=ts_["ffn_tm"], tf=ts_["ffn_tf"])
        p_rw, p_cv, p_mla, p_gate, p_vd = _inproj(x2, (g_mix, i), (w_cat, i), tm=ts_["tok_tm"])

        rw_wts = dict(at(rw_all), e64=e64)
        if i > 0:
            rw_wts.update(at(rw_res, i - 1))
        r, v, kn, lw, kd, bb, g, bn = _rw_prep(
            p_rw.reshape(B, S, RW_COLS), p_vd if i > 0 else None, v_first, rw_wts, ts=ts_["seq_ts"])
        if i == 0:
            v_first = v
        o_f, o_b = _rw_scan(r, v, kn, lw, kd, bb, tt=ts_["scan_tt"],
                            nb=2 if B % 2 == 0 else 1)

        ucv = _cv(p_cv.reshape(B, S, CV_COLS), *[(a, i) for a in cv_all], ts=ts_["seq_ts"])

        qh, kh, vh = _mla_prep(p_mla.reshape(B, S, MLA_COLS_PAD), tabs, at(mla_all), ts=ts_["seq_ts"])
        omla = _attn(qh, kh, vh, tq=ts_["attn_tq"])

        x2 = _merge(x2, o_f.reshape(T, -1), o_b.reshape(T, -1), g.reshape(T, -1), bn.reshape(T, -1),
                    ucv.reshape(T, -1), omla.reshape(T, -1), p_gate, dict(at(m_all), e64=e64),
                    tm=ts_["tok_tm"])

        x2 = _ffn(x2, *[(a, i) for a in ffn2], tm=ts_["ffn_tm"], tf=ts_["ffn_tf"])
    return x2.reshape(B, S, D)
```

```python
import functools
import math

import jax
import jax.numpy as jnp
from jax import lax
from jax.experimental import pallas as pl
from jax.experimental.pallas import tpu as pltpu

F32 = jnp.float32
BF16 = jnp.bfloat16

D_MODEL = 1024
D_FF = 2816
NORM_EPS = 1e-6

RW_HEADS = 8
RW_HEAD_DIM = 64
RW_WIDTH = RW_HEADS * RW_HEAD_DIM
RW_DECAY_RANK = 64
RW_ICL_RANK = 64
RW_VRES_RANK = 32
RW_GATE_RANK = 128
RW_GN_EPS = 64e-5
RW_COLS = 3 * RW_WIDTH + 2 * RW_DECAY_RANK + 2 * RW_ICL_RANK + RW_GATE_RANK

CV_CHANNELS = 512
CONV_TAPS = 31
CV_LN_EPS = 1e-5
CV_COLS = 2 * CV_CHANNELS

MLA_HEADS = 8
MLA_Q_RANK = 384
MLA_KV_RANK = 256
MLA_NOPE = 64
MLA_ROPE = 32
MLA_QK = MLA_NOPE + MLA_ROPE
MLA_V = 64
ROPE_THETA = 10000.0
MLA_COLS = MLA_Q_RANK + MLA_KV_RANK + MLA_ROPE
MLA_COLS_PAD = 768
GATE_COLS = 3 * D_MODEL
VD_COLS_PAD = 128

LANE = 128
VMEM_LIMIT = 56 * 1024 * 1024

RW_CHUNK = 64


def _cparams(sem):
    return pltpu.CompilerParams(dimension_semantics=sem, vmem_limit_bytes=VMEM_LIMIT)


def _wspec(w, **kw):
    arr, layer = w
    if layer is None:
        n = arr.ndim
        return pl.BlockSpec(arr.shape, lambda *_: (0,) * n, **kw)
    n = arr.ndim - 1
    return pl.BlockSpec((None,) + arr.shape[1:], lambda *_: (layer,) + (0,) * n, **kw)


def _mm(a, b):
    return jnp.dot(a.astype(BF16), b.astype(BF16), preferred_element_type=F32)


def _mm_nt(a, b):
    return lax.dot_general(a.astype(BF16), b.astype(BF16), (((1,), (1,)), ((), ())),
                           preferred_element_type=F32)


def _mm_tn(a, b):
    return lax.dot_general(a.astype(BF16), b.astype(BF16), (((0,), (0,)), ((), ())),
                           preferred_element_type=F32)


def _split3(x):
    hi = x.astype(BF16)
    r1 = x - hi.astype(F32)
    mid = r1.astype(BF16)
    lo = (r1 - mid.astype(F32)).astype(BF16)
    return hi, mid, lo


def _mm_x_exact(x, m):
    hi = x.astype(BF16)
    lo = (x - hi.astype(F32)).astype(BF16)
    return jnp.dot(hi, m, preferred_element_type=F32) + jnp.dot(lo, m, preferred_element_type=F32)


def _mm_exact_x(m, x):
    hi, mid, lo = _split3(x)
    return (jnp.dot(m, hi, preferred_element_type=F32) + jnp.dot(m, mid, preferred_element_type=F32)
            + jnp.dot(m, lo, preferred_element_type=F32))


def _sigmoid(x):
    return 1.0 / (1.0 + jnp.exp(-x))


def _rms(x, g, eps=NORM_EPS):
    return x * lax.rsqrt(jnp.mean(x * x, axis=-1, keepdims=True) + eps) * g


def _ffn_kernel(x_ref, g_ref, w1_ref, w3_ref, w2_ref, o_ref, h_ref, *, tf):
    x = x_ref[...]
    h_ref[...] = _rms(x, g_ref[...]).astype(BF16)
    o_ref[...] = x
    n_slab = w1_ref.shape[1] // tf

    def slab(j, carry):
        c = pl.ds(pl.multiple_of(j * tf, tf), tf)
        h = h_ref[...]
        a = jnp.dot(h, w1_ref[:, c], preferred_element_type=F32)
        b = jnp.dot(h, w3_ref[:, c], preferred_element_type=F32)
        u = (a * _sigmoid(a) * b).astype(BF16)
        o_ref[...] += 0.5 * jnp.dot(u, w2_ref[c, :], preferred_element_type=F32)
        return carry

    lax.fori_loop(0, n_slab, slab, 0)


def _ffn(x2, g, w1, w3, w2, *, tm, tf):
    T, D = x2.shape
    once = dict(pipeline_mode=pl.Buffered(1))
    return pl.pallas_call(
        functools.partial(_ffn_kernel, tf=tf),
        grid=(T // tm,),
        in_specs=[pl.BlockSpec((tm, D), lambda i: (i, 0)), _wspec(g),
                  _wspec(w1, **once), _wspec(w3, **once), _wspec(w2, **once)],
        out_specs=pl.BlockSpec((tm, D), lambda i: (i, 0)),
        out_shape=jax.ShapeDtypeStruct((T, D), F32),
        scratch_shapes=[pltpu.VMEM((tm, D), BF16)],
        compiler_params=_cparams(("parallel",)),
        name="ffn",
    )(x2, g[0], w1[0], w3[0], w2[0])


def _inproj_kernel(x_ref, g_ref, w_ref, rw_ref, cv_ref, mla_ref, gate_ref, vd_ref):
    h = _rms(x_ref[...], g_ref[...]).astype(BF16)
    c = 0
    for ref in (rw_ref, cv_ref, mla_ref, gate_ref, vd_ref):
        n = ref.shape[1]
        ref[...] = jnp.dot(h, w_ref[:, c:c + n], preferred_element_type=F32).astype(ref.dtype)
        c += n


def _inproj(x2, g, w_cat, *, tm):
    T, D = x2.shape
    widths = (RW_COLS, CV_COLS, MLA_COLS_PAD, GATE_COLS, VD_COLS_PAD)
    assert w_cat[0].shape[-1] == sum(widths)
    return pl.pallas_call(
        _inproj_kernel,
        grid=(T // tm,),
        in_specs=[pl.BlockSpec((tm, D), lambda i: (i, 0)), _wspec(g), _wspec(w_cat)],
        out_specs=[pl.BlockSpec((tm, n), lambda i: (i, 0)) for n in widths],
        out_shape=[jax.ShapeDtypeStruct((T, n), dt)
                   for n, dt in zip(widths, (F32, BF16, BF16, BF16, F32))],
        compiler_params=_cparams(("parallel",)),
        name="inproj",
    )(x2, g[0], w_cat[0])


def _rw_prep_kernel(*refs, has_vres):
    if has_vres:
        (p_ref, pp_ref, pn_ref, vd_ref, vf_ref, mu_ref, wup_ref, w0_ref, aup_ref, a0_ref, gup_ref,
         kk_ref, ka_ref, rk_ref, e_ref, v0_ref, vup_ref,
         r_o, v_o, kn_o, lw_o, kd_o, ic_o, g_o, bn_o) = refs
    else:
        (p_ref, pp_ref, pn_ref, mu_ref, wup_ref, w0_ref, aup_ref, a0_ref, gup_ref,
         kk_ref, ka_ref, rk_ref, e_ref,
         r_o, v_o, kn_o, lw_o, kd_o, ic_o, g_o, bn_o) = refs
    i = pl.program_id(1)
    n = pl.num_programs(1)
    C = RW_WIDTH
    p = p_ref[0]
    ts = p.shape[0]
    prev_row = jnp.where(i > 0, pp_ref[0, 7:8, :], 0.0)
    next_row = jnp.where(i < n - 1, pn_ref[0, 0:1, :], 0.0)
    rows = lax.broadcasted_iota(jnp.int32, (ts, 1), 0)
    prev = jnp.where(rows == 0, prev_row, pltpu.roll(p, 1, 0))
    nxt = jnp.where(rows == ts - 1, next_row, pltpu.roll(p, ts - 1, 0))
    pm = p + (0.5 * (prev + nxt) - p) * mu_ref[...]

    r = pm[:, 0:C]
    k = pm[:, C:2 * C]
    v = pm[:, 2 * C:3 * C]
    o1 = 3 * C
    wd = jnp.tanh(pm[:, o1:o1 + 2 * RW_DECAY_RANK])
    o2 = o1 + 2 * RW_DECAY_RANK
    ad = pm[:, o2:o2 + 2 * RW_ICL_RANK]
    o3 = o2 + 2 * RW_ICL_RANK
    gd = pm[:, o3:o3 + RW_GATE_RANK]

    wl = w0_ref[...] + _mm(wd, wup_ref[...])
    lw = (-math.exp(-0.5)) * _sigmoid(wl)
    icl = _sigmoid(a0_ref[...] + _mm(ad, aup_ref[...]))
    g = _mm(_sigmoid(gd), gup_ref[...])

    if has_vres:
        nu = _sigmoid(v0_ref[...] + _mm(vd_ref[...], vup_ref[...]))
        v = v + (vf_ref[0] - v) * nu

    e = e_ref[...]
    kkr = k * kk_ref[...]
    ss = _mm_x_exact(kkr * kkr, e)
    kn = kkr * lax.rsqrt(jnp.maximum(ss, 1e-24))

    ka = ka_ref[...]
    kd0 = k * (1.0 + (icl[:, 0:C] - 1.0) * ka)
    kd1 = k * (1.0 + (icl[:, C:2 * C] - 1.0) * ka)
    kb = 0.5 * (kd0 + kd1)
    sb = _mm_x_exact(r * kb * rk_ref[...], e)

    r_o[0] = r.astype(r_o.dtype)
    v_o[0] = v.astype(v_o.dtype)
    kn_o[0] = kn.astype(kn_o.dtype)
    lw_o[0, 0] = lw[:, 0:C]
    lw_o[0, 1] = lw[:, C:2 * C]
    kd_o[0, 0] = kd0.astype(kd_o.dtype)
    kd_o[0, 1] = kd1.astype(kd_o.dtype)
    ic_o[0, 0] = (kn * icl[:, 0:C]).astype(ic_o.dtype)
    ic_o[0, 1] = (kn * icl[:, C:2 * C]).astype(ic_o.dtype)
    g_o[0] = g.astype(g_o.dtype)
    bn_o[0] = (sb * v).astype(bn_o.dtype)


def _rw_prep(p_rw, vd, v_first, wts, *, ts):
    B, S, _ = p_rw.shape
    C = RW_WIDTH
    has_vres = vd is not None
    nblk8 = S // 8
    tile = lambda b, i: (b, i, 0)
    in_arrays = [p_rw, p_rw, p_rw]
    in_specs = [
        pl.BlockSpec((1, ts, RW_COLS), tile),
        pl.BlockSpec((1, 8, RW_COLS), lambda b, i: (b, jnp.maximum(i * (ts // 8) - 1, 0), 0)),
        pl.BlockSpec((1, 8, RW_COLS), lambda b, i: (b, jnp.minimum((i + 1) * (ts // 8), nblk8 - 1), 0)),
    ]
    if has_vres:
        in_arrays += [vd, v_first]
        in_specs += [pl.BlockSpec((ts, VD_COLS_PAD), lambda b, i: (b * (S // ts) + i, 0)),
                     pl.BlockSpec((1, ts, C), tile)]
    names = ["mu", "wup", "w0", "aup", "a0", "gup", "k_k", "k_a", "r_k", "e64"]
    if has_vres:
        names += ["v0", "vup"]
    for nme in names:
        in_arrays.append(wts[nme][0])
        in_specs.append(_wspec(wts[nme]))
    one = jax.ShapeDtypeStruct((B, S, C), BF16)
    two = jax.ShapeDtypeStruct((B, 2, S, C), BF16)
    two_f32 = jax.ShapeDtypeStruct((B, 2, S, C), F32)
    ospec1 = pl.BlockSpec((1, ts, C), tile)
    ospec2 = pl.BlockSpec((1, 2, ts, C), lambda b, i: (b, 0, i, 0))
    return pl.pallas_call(
        functools.partial(_rw_prep_kernel, has_vres=has_vres),
        grid=(B, S // ts),
        in_specs=in_specs,
        out_specs=[ospec1, ospec1, ospec1, ospec2, ospec2, ospec2, ospec1, ospec1],
        out_shape=[one, one, one, two_f32, two, two, one, one],
        compiler_params=_cparams(("parallel", "parallel")),
        name="rw_prep",
    )(*in_arrays)


def _rw_scan_kernel(rf_ref, vf_ref, knf_ref, lwf_ref, kdf_ref, bbf_ref,
                    rb_ref, vb_ref, knb_ref, lwb_ref, kdb_ref, bbb_ref,
                    of_ref, ob_ref, z_ref, *, n_chunks):
    j = pl.program_id(1)
    C = RW_CHUNK
    N = RW_HEAD_DIM
    P = RW_HEADS // 2
    W = 2 * N

    @pl.when(j == 0)
    def _():
        z_ref[...] = jnp.zeros_like(z_ref)

    row2 = lax.broadcasted_iota(jnp.int32, (2 * C, W), 0)
    col2 = lax.broadcasted_iota(jnp.int32, (2 * C, W), 1)
    order = (row2 % C) - (col2 % C)
    need = jnp.where(row2 >= C, 0, 1)
    gmask = (order >= need, -order >= need)
    left2 = (col2 < N) == (row2 < C)
    row1 = lax.broadcasted_iota(jnp.int32, (C, W), 0)
    col1 = lax.broadcasted_iota(jnp.int32, (C, W), 1)
    eye = jnp.where(col1 % N == row1, 1.0, 0.0)
    t1 = lax.broadcasted_iota(jnp.int32, (C, C), 0)
    i1 = lax.broadcasted_iota(jnp.int32, (C, C), 1)
    tri = (jnp.where(i1 <= t1, 1.0, 0.0).astype(BF16), jnp.where(i1 >= t1, 1.0, 0.0).astype(BF16))
    in_refs = ((rf_ref, vf_ref, knf_ref, lwf_ref, kdf_ref, bbf_ref),
               (rb_ref, vb_ref, knb_ref, lwb_ref, kdb_ref, bbb_ref))
    out_refs = (of_ref, ob_ref)
    slabs = [(bi, d) for bi in range(rf_ref.shape[0]) for d in range(2)]

    def bd(x):
        xb = x.astype(BF16)
        return jnp.where(left2, jnp.concatenate([xb, xb], axis=0), jnp.zeros((), BF16))

    def chunk(cc, carry):
        rows_d = (pl.ds(pl.multiple_of(cc * C, C), C),
                  pl.ds(pl.multiple_of((n_chunks - 1 - cc) * C, C), C))
        wide = []
        for bi, d in slabs:
            r_ref, v_ref, kn_ref, lw_ref, kd_ref, bb_ref = in_refs[d]
            rows = rows_d[d]
            r = r_ref[bi, rows, :].astype(F32)
            v = v_ref[bi, rows, :]
            kn = kn_ref[bi, rows, :].astype(F32)
            lw = lw_ref[bi, 0, rows, :]
            kd = kd_ref[bi, 0, rows, :].astype(F32)
            b = bb_ref[bi, 0, rows, :].astype(F32)
            lc = _mm_exact_x(tri[d], lw)
            ltot = jnp.sum(lw, axis=0, keepdims=True)
            e_nlc = jnp.exp(-lc)
            e_t = jnp.exp(ltot - lc)
            wide.append(dict(
                at=(-kn * jnp.exp(lc - lw)).astype(BF16), rt=(r * jnp.exp(lc)).astype(BF16),
                bt=(b * e_nlc).astype(BF16), kt=(kd * e_nlc).astype(BF16),
                bh=(b * e_t).astype(BF16), kh=(kd * e_t).astype(BF16),
                v=v.astype(BF16), gam=jnp.exp(ltot)))
        chains = [(s, p) for s in range(len(slabs)) for p in range(P)]
        ps = lambda s, p, name: wide[s][name][:, p * W:(p + 1) * W]
        mm = lambda a, bmat: jnp.dot(a.astype(BF16), bmat, preferred_element_type=F32)
        dmask = lambda s: gmask[slabs[s][1]]

        x = [jnp.concatenate([ps(d, p, "at"), ps(d, p, "rt")], axis=0) for d, p in chains]
        gb = [jnp.where(dmask(d), _mm_nt(xi, bd(ps(d, p, "bt"))), 0.0) for xi, (d, p) in zip(x, chains)]
        gk = [jnp.where(dmask(d), _mm_nt(xi, bd(ps(d, p, "kt"))), 0.0) for xi, (d, p) in zip(x, chains)]
        l1 = [g[:C] for g in gb]
        tinv = [eye + l for l in l1]
        lp = [mm(l, bd(l)) for l in l1]
        n_lvl = int(math.log2(C))
        for lvl in range(2, n_lvl):
            both = [mm(jnp.concatenate([l, t], axis=0), bd(l)) for l, t in zip(lp, tinv)]
            tinv = [t + bo[C:] for t, bo in zip(tinv, both)]
            lp = [bo[:C] for bo in both]
        tinv = [t + mm(t, bd(l)) for t, l in zip(tinv, lp)]
        ah = [mm(t, bd(ps(d, p, "at"))) for t, (d, p) in zip(tinv, chains)]
        w = [mm(g[:C], bd(ps(d, p, "v"))) for g, (d, p) in zip(gk, chains)]
        vh = [mm(t, bd(wi)) for t, wi in zip(tinv, w)]
        z = [z_ref[n] for n in range(len(chains))]
        az = [_mm_nt(jnp.concatenate([a.astype(BF16), ps(d, p, "rt")], axis=0), zi)
              for a, zi, (d, p) in zip(ah, z, chains)]
        u = [(azi[:C] + vhi).astype(BF16) for azi, vhi in zip(az, vh)]
        o = [azi[C:] + mm(g1[C:], bd(ui)) + mm(g2[C:], bd(ps(d, p, "v")))
             for azi, g1, g2, ui, (d, p) in zip(az, gb, gk, u, chains)]
        znew = [zi * wide[d]["gam"][:, p * W:(p + 1) * W]
                + jnp.where(left2, _mm_tn(jnp.concatenate([ui, ps(d, p, "v")], axis=0),
                                          jnp.concatenate([ps(d, p, "bh"), ps(d, p, "kh")], axis=0)), 0.0)
                for zi, ui, (d, p) in zip(z, u, chains)]
        z_ref[...] = jnp.stack(znew, axis=0)
        for s, (bi, d) in enumerate(slabs):
            out_refs[d][bi, rows_d[d], :] = jnp.concatenate(o[s * P:(s + 1) * P], axis=1).astype(
                out_refs[d].dtype)
        return carry

    lax.fori_loop(0, n_chunks, chunk, 0, unroll=2)


def _rw_scan(r, v, kn, lw, kd, bb, *, tt, nb):
    B, S, C = r.shape
    nT = S // tt
    f1 = pl.BlockSpec((nb, tt, C), lambda b, j: (b, j, 0))
    b1 = pl.BlockSpec((nb, tt, C), lambda b, j: (b, nT - 1 - j, 0))
    f2 = pl.BlockSpec((nb, 1, tt, C), lambda b, j: (b, 0, j, 0))
    b2 = pl.BlockSpec((nb, 1, tt, C), lambda b, j: (b, 1, nT - 1 - j, 0))
    out = jax.ShapeDtypeStruct((B, S, C), BF16)
    return pl.pallas_call(
        functools.partial(_rw_scan_kernel, n_chunks=tt // RW_CHUNK),
        grid=(B // nb, nT),
        in_specs=[f1, f1, f1, f2, f2, f2, b1, b1, b1, b2, b2, b2],
        out_specs=[f1, b1],
        out_shape=[out, out],
        scratch_shapes=[pltpu.VMEM((nb * RW_HEADS, 2 * RW_HEAD_DIM, 2 * RW_HEAD_DIM), F32)],
        compiler_params=_cparams(("parallel", "arbitrary")),
        name="rw_scan",
    )(r, v, kn, lw, kd, bb, r, v, kn, lw, kd, bb)


CV_HALO = 16


def _cv_kernel(p_ref, pp_ref, pn_ref, dw_ref, db_ref, lg_ref, lb_ref, o_ref, u_ref):
    i = pl.program_id(1)
    n = pl.num_programs(1)
    ts = p_ref.shape[1]
    Cc = CV_CHANNELS

    def glu(x):
        x = x.astype(F32)
        return x[:, :Cc] * _sigmoid(x[:, Cc:])

    u_ref[CV_HALO:CV_HALO + ts, :] = glu(p_ref[0])
    u_ref[0:CV_HALO, :] = jnp.where(i > 0, glu(pp_ref[0]), 0.0)
    u_ref[CV_HALO + ts:, :] = jnp.where(i < n - 1, glu(pn_ref[0]), 0.0)
    base = CV_HALO - CONV_TAPS // 2
    ext = ts + 8
    acc = None
    for res in range(8):
        part = None
        for s in range(res, base + CONV_TAPS, 8):
            if s < base:
                continue
            term = u_ref[s - res:s - res + ext, :] * dw_ref[s - base:s - base + 1, :]
            part = term if part is None else part + term
        if part is None:
            continue
        shifted = part[:ts] if res == 0 else pltpu.roll(part, ext - res, 0)[:ts]
        acc = shifted if acc is None else acc + shifted
    acc = acc + db_ref[...]
    mu = jnp.mean(acc, axis=-1, keepdims=True)
    xc = acc - mu
    var = jnp.mean(xc * xc, axis=-1, keepdims=True)
    y = xc * lax.rsqrt(var + CV_LN_EPS) * lg_ref[...] + lb_ref[...]
    o_ref[0] = (y * _sigmoid(y)).astype(o_ref.dtype)


def _cv(p_cv, dw, db, lg, lb, *, ts):
    B, S, _ = p_cv.shape
    nh = S // CV_HALO
    return pl.pallas_call(
        _cv_kernel,
        grid=(B, S // ts),
        in_specs=[
            pl.BlockSpec((1, ts, CV_COLS), lambda b, i: (b, i, 0)),
            pl.BlockSpec((1, CV_HALO, CV_COLS),
                         lambda b, i: (b, jnp.maximum(i * (ts // CV_HALO) - 1, 0), 0)),
            pl.BlockSpec((1, CV_HALO, CV_COLS),
                         lambda b, i: (b, jnp.minimum((i + 1) * (ts // CV_HALO), nh - 1), 0)),
            _wspec(dw), _wspec(db), _wspec(lg), _wspec(lb),
        ],
        out_specs=pl.BlockSpec((1, ts, CV_CHANNELS), lambda b, i: (b, i, 0)),
        out_shape=jax.ShapeDtypeStruct((B, S, CV_CHANNELS), BF16),
        scratch_shapes=[pltpu.VMEM((ts + 2 * CV_HALO, CV_CHANNELS), F32)],
        compiler_params=_cparams(("parallel", "parallel")),
        name="cv_branch",
    )(p_cv, p_cv, p_cv, dw[0], db[0], lg[0], lb[0])


def _mla_prep_kernel(p_ref, cb_ref, s1_ref, s2_ref, qn_ref, wq_ref, wqs_ref, kvn_ref, wkv_ref,
                     gq_ref, gqs_ref, gk_ref, q_o, k_o, v_o):
    p = p_ref[0].astype(F32)
    cb = cb_ref[0]
    s1 = s1_ref[0]
    s2 = s2_ref[0]
    s12 = s1 + s2
    cq = p[:, 0:MLA_Q_RANK]
    ckv = p[:, MLA_Q_RANK:MLA_Q_RANK + MLA_KV_RANK]
    kr = p[:, MLA_Q_RANK + MLA_KV_RANK:MLA_COLS_PAD]
    cqn = _rms(cq, qn_ref[...]).astype(BF16)
    qall = jnp.dot(cqn, wq_ref[...], preferred_element_type=F32)
    qswp = jnp.dot(cqn, wqs_ref[...], preferred_element_type=F32)
    kvall = _mm(_rms(ckv, kvn_ref[...]), wkv_ref[...])
    gq = gq_ref[...]
    gqs = gqs_ref[...]
    gk = gk_ref[...]
    lane = lax.broadcasted_iota(jnp.int32, (1, LANE), 1)
    ones_col = jnp.where(lane == MLA_V, 1.0, 0.0)
    scale = MLA_QK ** -0.5 * math.log2(math.e)
    HB = MLA_HEADS * LANE

    krot = pltpu.roll(kr, MLA_NOPE, 1)
    kr_ss = jnp.sum(krot * krot, axis=-1, keepdims=True)
    krg = krot * gk
    kr_r = (krg * cb + pltpu.roll(krg, LANE - MLA_ROPE // 2, 1) * s1
            + pltpu.roll(krg, MLA_ROPE // 2, 1) * s2)

    for h in range(MLA_HEADS):
        blk = slice(h * LANE, (h + 1) * LANE)
        qb = qall[:, blk]
        rq = lax.rsqrt(jnp.sum(qb * qb, axis=-1, keepdims=True) * (1.0 / MLA_QK) + NORM_EPS)
        q_o[0, h] = (((qb * gq) * cb + (qswp[:, blk] * gqs) * s12) * (rq * scale)).astype(q_o.dtype)
        kb = kvall[:, blk]
        rk = lax.rsqrt((jnp.sum(kb * kb, axis=-1, keepdims=True) + kr_ss) * (1.0 / MLA_QK) + NORM_EPS)
        k_o[0, h] = ((kb * gk + kr_r) * rk).astype(k_o.dtype)
        v_o[0, h] = (kvall[:, HB + h * LANE:HB + (h + 1) * LANE] + ones_col).astype(v_o.dtype)


def _mla_prep(p_mla, tabs, wts, *, ts):
    B, S, _ = p_mla.shape
    H = MLA_HEADS
    tile = lambda b, i: (b, i, 0)
    tspec = pl.BlockSpec((1, ts, LANE), tile)
    names = ["q_norm", "wq", "wq_swap", "kv_norm", "wkv", "gq", "gq_swap", "gk"]
    hspec = pl.BlockSpec((1, H, ts, LANE), lambda b, i: (b, 0, i, 0))
    hshape = jax.ShapeDtypeStruct((B, H, S, LANE), BF16)
    return pl.pallas_call(
        _mla_prep_kernel,
        grid=(B, S // ts),
        in_specs=[pl.BlockSpec((1, ts, MLA_COLS_PAD), tile), tspec, tspec, tspec]
        + [_wspec(wts[n]) for n in names],
        out_specs=[hspec, hspec, hspec],
        out_shape=[hshape, hshape, hshape],
        compiler_params=_cparams(("parallel", "parallel")),
        name="mla_prep",
    )(p_mla, *tabs, *[wts[n][0] for n in names])


ATTN_SUB = 512


def _attn_kernel(q_ref, k_ref, v_ref, o_ref):
    tq = q_ref.shape[2]
    items = [(r0, h) for r0 in range(0, tq, ATTN_SUB) for h in range(2)]
    scores = lambda r0, h: _mm_nt(q_ref[0, h, r0:r0 + ATTN_SUB, :], k_ref[0, h])
    s = scores(*items[0])
    outs = []
    for n, (r0, h) in enumerate(items):
        s_next = scores(*items[n + 1]) if n + 1 < len(items) else None
        m = jnp.max(s, axis=-1, keepdims=True)
        p = jnp.exp2(s - m).astype(BF16)
        ov = jnp.dot(p, v_ref[0, h], preferred_element_type=F32)
        outs.append(ov[:, :MLA_V] / ov[:, MLA_V:MLA_V + 1])
        if h == 1:
            o_ref[0, r0:r0 + ATTN_SUB, :] = jnp.concatenate(outs, axis=1).astype(o_ref.dtype)
            outs = []
        s = s_next


def _attn(q, k, v, *, tq):
    B, H, S, _ = q.shape
    return pl.pallas_call(
        _attn_kernel,
        grid=(B, H // 2, S // tq),
        in_specs=[
            pl.BlockSpec((1, 2, tq, LANE), lambda b, h, i: (b, h, i, 0)),
            pl.BlockSpec((1, 2, S, LANE), lambda b, h, i: (b, h, 0, 0)),
            pl.BlockSpec((1, 2, S, LANE), lambda b, h, i: (b, h, 0, 0)),
        ],
        out_specs=pl.BlockSpec((1, tq, 2 * MLA_V), lambda b, h, i: (b, i, h)),
        out_shape=jax.ShapeDtypeStruct((B, S, H * MLA_V), BF16),
        compiler_params=_cparams(("parallel", "parallel", "parallel")),
        name="mla_attn",
    )(q, k, v)


def _merge_kernel(x_ref, of_ref, ob_ref, g_ref, bn_ref, ucv_ref, omla_ref, gate_ref,
                  avg_ref, gng_ref, gnb_ref, wrw_ref, wcv_ref, wmla_ref, wo_ref, out_ref):
    D = D_MODEL
    o = of_ref[...].astype(F32) + ob_ref[...].astype(F32)
    avg = avg_ref[...]
    inv_n = 1.0 / RW_HEAD_DIM
    mu = _mm(o, avg) * inv_n
    xc = o - mu
    var = _mm(xc * xc, avg) * inv_n
    on = xc * lax.rsqrt(var + RW_GN_EPS) * gng_ref[...] + gnb_ref[...]
    orw = (on + bn_ref[...].astype(F32)) * g_ref[...].astype(F32)
    gate = gate_ref[...].astype(F32)
    merged = (_sigmoid(gate[:, 0:D]) * _mm(orw, wrw_ref[...])
              + _sigmoid(gate[:, D:2 * D]) * jnp.dot(ucv_ref[...], wcv_ref[...], preferred_element_type=F32)
              + _sigmoid(gate[:, 2 * D:3 * D]) * jnp.dot(omla_ref[...], wmla_ref[...],
                                                         preferred_element_type=F32))
    out_ref[...] = x_ref[...] + _mm(merged, wo_ref[...])


def _merge(x2, o_f, o_b, g, bn, ucv, omla, gate, wts, *, tm):
    T, D = x2.shape
    C = RW_WIDTH
    row = lambda n: pl.BlockSpec((tm, n), lambda i: (i, 0))
    names = ["e64", "gn_g", "gn_b", "w_rw", "w_cv", "w_mla", "w_o"]
    return pl.pallas_call(
        _merge_kernel,
        grid=(T // tm,),
        in_specs=[row(D), row(C), row(C), row(C), row(C), row(C), row(C),
                  row(GATE_COLS)] + [_wspec(wts[n]) for n in names],
        out_specs=row(D),
        out_shape=jax.ShapeDtypeStruct((T, D), F32),
        compiler_params=_cparams(("parallel",)),
        name="merge",
    )(x2, o_f, o_b, g, bn, ucv, omla, gate, *[wts[n][0] for n in names])


def _block_diag2(w):
    z = jnp.zeros_like(w[:, 0])
    return jnp.concatenate([jnp.concatenate([w[:, 0], z], axis=2),
                            jnp.concatenate([z, w[:, 1]], axis=2)], axis=1)


def _head_block_cols(w):
    Lw, R, H, width = w.shape
    return jnp.pad(w, ((0, 0), (0, 0), (0, 0), (0, LANE - width))).reshape(Lw, R, H * LANE)


def _rope_tables(positions):
    half = MLA_ROPE // 2
    inv = ROPE_THETA ** (-jnp.arange(0, MLA_ROPE, 2, dtype=F32) / MLA_ROPE)
    ang = positions.astype(F32)[..., None] * inv
    cos, sin = jnp.cos(ang), jnp.sin(ang)
    B, S = positions.shape
    one = jnp.ones((B, S, MLA_NOPE), F32)
    z64 = jnp.zeros((B, S, MLA_NOPE), F32)
    z16 = jnp.zeros((B, S, half), F32)
    pad = jnp.zeros((B, S, LANE - MLA_QK), F32)
    cb = jnp.concatenate([one, cos, cos, pad + 1.0], axis=-1)
    s1 = jnp.concatenate([z64, -sin, z16, pad], axis=-1)
    s2 = jnp.concatenate([z64, z16, sin, pad], axis=-1)
    return cb, s1, s2


def _head_ones():
    hid = jnp.arange(RW_WIDTH) // RW_HEAD_DIM
    return (hid[:, None] == hid[None, :]).astype(BF16)


def _tile_sizes(T, S):
    pick = lambda n, cands: next(c for c in cands if n % c == 0)
    return dict(
        ffn_tm=pick(T, (1024, 512, 256, 128, 64, 32, 16, 8)),
        ffn_tf=pick(D_FF, (256, 128)),
        tok_tm=pick(T, (512, 256, 128, 64, 32, 16, 8)),
        seq_ts=pick(S, (512, 256, 128, 64)),
        scan_tt=pick(S, (512, 256, 128, 64)),
        attn_tq=pick(S, (2048, 1024, 512, 256)),
    )


def kernel(x, positions, norm_ffn1, ffn1_w1, ffn1_w3, ffn1_w2, norm_mix, w_in, rw_mu, rw_w0, rw_w_up, rw_a0, rw_a_up, rw_g_up, rw_k_k, rw_k_a, rw_r_k, rw_v0, rw_v_down, rw_v_up, rw_gn_g, rw_gn_b, rw_w_branch, cv_dw_w, cv_dw_b, cv_ln_g, cv_ln_b, cv_w_branch, mla_q_norm, mla_w_uq, mla_kv_norm, mla_w_ukv, mla_qk_q_g, mla_qk_k_g, mla_w_branch, w_o, norm_ffn2, ffn2_w1, ffn2_w3, ffn2_w2):
    B, S, D = x.shape
    T = B * S
    L = w_in.shape[0]
    ts_ = _tile_sizes(T, S)
    tabs = _rope_tables(positions)
    x2 = x.reshape(T, D)

    rows = lambda a: a.reshape(a.shape[0], 1, -1)
    bf = lambda a: a.astype(BF16)
    e64 = (_head_ones(), None)
    c1, c2, c3 = RW_COLS, RW_COLS + CV_COLS, RW_COLS + CV_COLS + MLA_COLS
    pad_last = lambda a, n: jnp.pad(a, [(0, 0)] * (a.ndim - 1) + [(0, n)])
    vdw = jnp.concatenate([jnp.zeros((1, D, RW_VRES_RANK), F32), rw_v_down], axis=0)
    w_cat = bf(jnp.concatenate([
        w_in[:, :, :c2], pad_last(w_in[:, :, c2:c3], MLA_COLS_PAD - MLA_COLS), w_in[:, :, c3:],
        pad_last(vdw, VD_COLS_PAD - RW_VRES_RANK)], axis=2))
    ffn1 = (rows(norm_ffn1), bf(ffn1_w1), bf(ffn1_w3), bf(ffn1_w2))
    ffn2 = (rows(norm_ffn2), bf(ffn2_w1), bf(ffn2_w3), bf(ffn2_w2))
    g_mix = rows(norm_mix)
    rw_all = dict(
        mu=rows(rw_mu), wup=bf(_block_diag2(rw_w_up)), w0=rows(rw_w0), aup=bf(_block_diag2(rw_a_up)),
        a0=rows(rw_a0), gup=bf(rw_g_up), k_k=rows(rw_k_k), k_a=rows(rw_k_a), r_k=rows(rw_r_k))
    rw_res = dict(v0=rows(rw_v0), vup=bf(jnp.pad(rw_v_up, ((0, 0), (0, VD_COLS_PAD - RW_VRES_RANK), (0, 0)))))
    cv_all = (cv_dw_w, rows(cv_dw_b), rows(cv_ln_g), rows(cv_ln_b))
    half = MLA_ROPE // 2
    swap = lambda a: jnp.concatenate(
        [a[..., :MLA_NOPE], a[..., MLA_NOPE + half:], a[..., MLA_NOPE:MLA_NOPE + half]], axis=-1)
    wq = mla_w_uq.reshape(L, MLA_Q_RANK, MLA_HEADS, MLA_QK)
    wkv = mla_w_ukv.reshape(L, MLA_KV_RANK, MLA_HEADS, MLA_NOPE + MLA_V)
    gpad = lambda gv: rows(pad_last(gv, LANE - MLA_QK))
    mla_all = dict(
        q_norm=rows(mla_q_norm), wq=bf(_head_block_cols(wq)), wq_swap=bf(_head_block_cols(swap(wq))),
        kv_norm=rows(mla_kv_norm),
        wkv=bf(jnp.concatenate([_head_block_cols(wkv[..., :MLA_NOPE]), _head_block_cols(wkv[..., MLA_NOPE:])],
                               axis=2)),
        gq=gpad(mla_qk_q_g), gq_swap=gpad(swap(mla_qk_q_g)), gk=gpad(mla_qk_k_g))
    m_all = dict(gn_g=rows(rw_gn_g), gn_b=rows(rw_gn_b), w_rw=bf(rw_w_branch), w_cv=bf(cv_w_branch),
                 w_mla=bf(mla_w_branch), w_o=bf(w_o))

    v_first = None
    for i in range(L):
        at = lambda d, layer=i: {k: (a, layer) for k, a in d.items()}
        x2 = _ffn(x2, *[(a, i) for a in ffn1], tm=ts_["ffn_tm"], tf=ts_["ffn_tf"])
        p_rw, p_cv, p_mla, p_gate, p_vd = _inproj(x2, (g_mix, i), (w_cat, i), tm=ts_["tok_tm"])

        rw_wts = dict(at(rw_all), e64=e64)
        if i > 0:
            rw_wts.update(at(rw_res, i - 1))
        r, v, kn, lw, kd, bb, g, bn = _rw_prep(
            p_rw.reshape(B, S, RW_COLS), p_vd if i > 0 else None, v_first, rw_wts, ts=ts_["seq_ts"])
        if i == 0:
            v_first = v
        o_f, o_b = _rw_scan(r, v, kn, lw, kd, bb, tt=ts_["scan_tt"],
                            nb=2 if B % 2 == 0 else 1)

        ucv = _cv(p_cv.reshape(B, S, CV_COLS), *[(a, i) for a in cv_all], ts=ts_["seq_ts"])

        qh, kh, vh = _mla_prep(p_mla.reshape(B, S, MLA_COLS_PAD), tabs, at(mla_all), ts=ts_["seq_ts"])
        omla = _attn(qh, kh, vh, tq=ts_["attn_tq"])

        x2 = _merge(x2, o_f.reshape(T, -1), o_b.reshape(T, -1), g.reshape(T, -1), bn.reshape(T, -1),
                    ucv.reshape(T, -1), omla.reshape(T, -1), p_gate, dict(at(m_all), e64=e64),
                    tm=ts_["tok_tm"])

        x2 = _ffn(x2, *[(a, i) for a in ffn2], tm=ts_["ffn_tm"], tf=ts_["ffn_tf"])
    return x2.reshape(B, S, D)
```
